```python
import jax, jax.numpy as jnp
from jax import lax
import numpy as np

D_MODEL = 4096
BATCH = 2
SEQ = 8192
DEPTH = 1

CHUNK = 64
NORM_EPS = 1e-6
D_LRU = D_MODEL
LRU_HEADS = 16
LRU_HEAD_DIM = D_LRU // LRU_HEADS
CONV_WIDTH = 4
LRU_C = 8.0
D_POOL = D_MODEL // 2
POOL_WINDOWS = (2, 4, 8, 16)
POOL_GROUPS = len(POOL_WINDOWS)
POOL_GROUP_DIM = D_POOL // POOL_GROUPS
POOL_OUT_DIM = D_MODEL // POOL_GROUPS
N_BRANCHES = 2
COL_LRU_X = 0
COL_LRU_GATE = D_LRU
COL_POOL = 2 * D_LRU
COL_MERGE = 2 * D_LRU + D_POOL
D_IN_PROJ = COL_MERGE + N_BRANCHES * D_MODEL
N_EXPERT_GROUPS = 8
EXPERTS_PER_GROUP = 8
N_EXPERTS = N_EXPERT_GROUPS * EXPERTS_PER_GROUP
TOP_K = 2
D_EXPERT = 512
EXPERT_BLOCK = 128

kernel_name = 'hawk_pool_hier_moe_hybrid'


def rms_norm(x, g):
    xf = x.astype(jnp.float32)
    y = xf * lax.rsqrt(jnp.mean(xf * xf, axis=-1, keepdims=True) + NORM_EPS)
    return (y * g.astype(jnp.float32)).astype(x.dtype)


def causal_depthwise_conv(u, w, b):
    c = u.shape[-1]
    out = lax.conv_general_dilated(
        u, w.reshape(CONV_WIDTH, 1, c).astype(u.dtype), window_strides=(1,),
        padding=[(CONV_WIDTH - 1, 0)], dimension_numbers=('NWC', 'WIO', 'NWC'),
        feature_group_count=c)
    return out + b


def rg_lru(u, w_r, b_r, w_i, b_i, lam):
    bsz, s, _ = u.shape
    uh = u.reshape(bsz, s, LRU_HEADS, LRU_HEAD_DIM)
    r = jax.nn.sigmoid(jnp.einsum('bshd,hde->bshe', uh, w_r).reshape(bsz, s, D_LRU) + b_r)
    i = jax.nn.sigmoid(jnp.einsum('bshd,hde->bshe', uh, w_i).reshape(bsz, s, D_LRU) + b_i)
    log_a = -LRU_C * r.astype(jnp.float32) * jax.nn.softplus(-lam.astype(jnp.float32))
    a = jnp.exp(log_a)
    beta = jnp.sqrt(-jnp.expm1(2.0 * log_a))
    bx = beta * (i * u).astype(jnp.float32)

    def combine(left, right):
        a_l, b_l = left
        a_r, b_rr = right
        return a_l * a_r, a_r * b_l + b_rr

    _, h = lax.associative_scan(combine, (a, bx), axis=1)
    return h.astype(u.dtype)


def multiscale_pool(v):
    s = v.shape[1]
    vf = v.astype(jnp.float32)
    c0 = jnp.pad(jnp.cumsum(vf, axis=1), ((0, 0), (1, 0), (0, 0)))
    pos1 = jnp.arange(1, s + 1, dtype=jnp.int32)
    outs = []
    for g, w in enumerate(POOL_WINDOWS):
        lo, hi = g * POOL_GROUP_DIM, (g + 1) * POOL_GROUP_DIM
        cg = c0[..., lo:hi]
        lagged = jnp.pad(cg[:, :s - w + 1], ((0, 0), (w - 1, 0), (0, 0)))
        count = jnp.minimum(pos1, w).astype(jnp.float32)[None, :, None]
        outs.append((cg[:, 1:] - lagged) / count - vf[..., lo:hi])
    return jnp.stack(outs, axis=2).astype(v.dtype)


def hierarchical_moe(h, wg, bg, we, be, w_gu, w_down):
    bsz, s, d = h.shape
    n_tok = bsz * s
    hf = h.reshape(n_tok, d)
    g_probs = jax.nn.softmax((hf @ wg).astype(jnp.float32) + bg.astype(jnp.float32), axis=-1)
    g_p, g_idx = lax.top_k(g_probs, 1)
    e_logits = ((hf @ we).astype(jnp.float32) + be.astype(jnp.float32)).reshape(
        n_tok, N_EXPERT_GROUPS, EXPERTS_PER_GROUP)
    e_logits = jnp.take_along_axis(e_logits, g_idx[:, :, None], axis=1)[:, 0]
    e_p, e_local = lax.top_k(jax.nn.softmax(e_logits, axis=-1), TOP_K)
    weights = g_p * e_p / jnp.sum(e_p, axis=-1, keepdims=True)
    expert_id = g_idx * EXPERTS_PER_GROUP + e_local

    n_assign = n_tok * TOP_K
    e_flat = expert_id.reshape(-1)
    tok_flat = jnp.repeat(jnp.arange(n_tok, dtype=jnp.int32), TOP_K)
    w_flat = weights.reshape(-1)
    order = jnp.argsort(e_flat)
    e_sorted = e_flat[order]
    counts = jnp.bincount(e_flat, length=N_EXPERTS).astype(jnp.int32)
    padded = (counts + EXPERT_BLOCK - 1) // EXPERT_BLOCK * EXPERT_BLOCK
    offsets = jnp.cumsum(counts) - counts
    padded_ends = jnp.cumsum(padded)
    padded_offsets = padded_ends - padded
    dest = padded_offsets[e_sorted] + jnp.arange(n_assign, dtype=jnp.int32) - offsets[e_sorted]
    max_rows = n_assign + N_EXPERTS * EXPERT_BLOCK
    n_blocks = max_rows // EXPERT_BLOCK
    row_tok = jnp.zeros((max_rows,), jnp.int32).at[dest].set(tok_flat[order])
    row_w = jnp.zeros((max_rows,), jnp.float32).at[dest].set(w_flat[order])
    block_start = jnp.arange(n_blocks, dtype=jnp.int32) * EXPERT_BLOCK
    block_expert = jnp.minimum(jnp.searchsorted(padded_ends, block_start, side='right'),
                               N_EXPERTS - 1)
    x_blocks = hf[row_tok].reshape(n_blocks, EXPERT_BLOCK, d)

    def expert_block(args):
        e, xb = args
        gate, up = jnp.split(xb @ w_gu[e], 2, axis=-1)
        return (jax.nn.silu(gate) * up) @ w_down[e]

    y_rows = lax.map(expert_block, (block_expert, x_blocks)).reshape(max_rows, d)
    y_rows = y_rows * row_w[:, None].astype(h.dtype)
    y = jnp.zeros_like(hf).at[row_tok].add(y_rows)
    return y.reshape(bsz, s, d)


def setup_inputs(seed: int = 0) -> dict:
    key = jax.random.key(seed)
    ks = jax.random.split(key, 24)
    f32 = jnp.float32

    def nrm(k, shape, fan_in):
        return jax.random.normal(k, shape, f32) * (fan_in ** -0.5)

    def small(k, shape, scale=0.02):
        return jax.random.normal(k, shape, f32) * scale

    a0 = jax.random.uniform(ks[10], (DEPTH, D_LRU), f32, minval=0.9, maxval=0.999)
    return {
        'x': jax.random.normal(ks[0], (BATCH, SEQ, D_MODEL), f32),
        'norm_mix_g': 1.0 + small(ks[1], (DEPTH, D_MODEL)),
        'w_in': nrm(ks[2], (DEPTH, D_MODEL, D_IN_PROJ), D_MODEL),
        'b_gate': small(ks[3], (DEPTH, N_BRANCHES * D_MODEL)),
        'conv_w': nrm(ks[4], (DEPTH, CONV_WIDTH, D_LRU), CONV_WIDTH),
        'conv_b': small(ks[5], (DEPTH, D_LRU)),
        'lru_wr': nrm(ks[6], (DEPTH, LRU_HEADS, LRU_HEAD_DIM, LRU_HEAD_DIM), LRU_HEAD_DIM),
        'lru_br': small(ks[7], (DEPTH, D_LRU)),
        'lru_wi': nrm(ks[8], (DEPTH, LRU_HEADS, LRU_HEAD_DIM, LRU_HEAD_DIM), LRU_HEAD_DIM),
        'lru_bi': small(ks[9], (DEPTH, D_LRU)),
        'lru_lambda': jnp.log(a0) - jnp.log1p(-a0),
        'w_lru_out': nrm(ks[11], (DEPTH, D_LRU, D_MODEL), D_LRU),
        'pool_w': nrm(ks[12], (DEPTH, POOL_GROUPS, POOL_GROUP_DIM, POOL_OUT_DIM), POOL_GROUP_DIM),
        'pool_scale': 1.0 + small(ks[13], (DEPTH, D_MODEL)),
        'w_out': nrm(ks[14], (DEPTH, D_MODEL, D_MODEL), D_MODEL),
        'norm_ffn_g': 1.0 + small(ks[15], (DEPTH, D_MODEL)),
        'router_wg': nrm(ks[16], (DEPTH, D_MODEL, N_EXPERT_GROUPS), D_MODEL),
        'router_bg': small(ks[17], (DEPTH, N_EXPERT_GROUPS), 0.01),
        'router_we': nrm(ks[18], (DEPTH, D_MODEL, N_EXPERTS), D_MODEL),
        'router_be': small(ks[19], (DEPTH, N_EXPERTS), 0.01),
        'exp_w_gu': nrm(ks[20], (DEPTH, N_EXPERTS, D_MODEL, 2 * D_EXPERT), D_MODEL),
        'exp_w_down': nrm(ks[21], (DEPTH, N_EXPERTS, D_EXPERT, D_MODEL), D_EXPERT),
        'norm_final_g': 1.0 + small(ks[22], (D_MODEL,)),
    }


def reference(x, norm_mix_g, w_in, b_gate, conv_w, conv_b, lru_wr, lru_br, lru_wi, lru_bi,
              lru_lambda, w_lru_out, pool_w, pool_scale, w_out, norm_ffn_g, router_wg,
              router_bg, router_we, router_be, exp_w_gu, exp_w_down, norm_final_g):
    bsz, s, _ = x.shape
    for l in range(DEPTH):
        xn = rms_norm(x, norm_mix_g[l])
        proj = xn @ w_in[l]
        u = proj[..., COL_LRU_X:COL_LRU_GATE]
        gate_lru = proj[..., COL_LRU_GATE:COL_POOL]
        v = proj[..., COL_POOL:COL_MERGE]
        merge_gates = jax.nn.sigmoid(proj[..., COL_MERGE:] + b_gate[l])

        uc = causal_depthwise_conv(u, conv_w[l], conv_b[l])
        h = rg_lru(uc, lru_wr[l], lru_br[l], lru_wi[l], lru_bi[l], lru_lambda[l])
        y_a = (h * jax.nn.gelu(gate_lru)) @ w_lru_out[l]

        pooled = multiscale_pool(v)
        y_b = jnp.einsum('bsgd,gde->bsge', pooled, pool_w[l]).reshape(bsz, s, D_MODEL)
        y_b = y_b * pool_scale[l]

        merged = merge_gates[..., :D_MODEL] * y_a + merge_gates[..., D_MODEL:] * y_b
        x = x + merged @ w_out[l]

        x = x + hierarchical_moe(rms_norm(x, norm_ffn_g[l]), router_wg[l], router_bg[l],
                                 router_we[l], router_be[l], exp_w_gu[l], exp_w_down[l])
    return rms_norm(x, norm_final_g)
```

```python
import functools

import jax
import jax.numpy as jnp
from jax import lax
from jax.experimental import pallas as pl
from jax.experimental.pallas import tpu as pltpu

NORM_EPS = 1e-6
LRU_C = 8.0
POOL_WINDOWS = (2, 4, 8, 16)
TOP_K = 2

VMEM_LIMIT_BYTES = 56 * 1024 * 1024
SUBLANES = 8
LANES = 128

MM_TM = 1024
MM_TN = 512
NORM_TM = 512
LRU_T = 512
POOL_T = 512
ROUTER_TM = 256
EXPERT_BLOCK = 256
COMBINE_TM = 128

f32 = jnp.float32
bf16 = jnp.bfloat16


def _cparams(*sem):
    return pltpu.CompilerParams(dimension_semantics=sem, vmem_limit_bytes=VMEM_LIMIT_BYTES)


def _rmsnorm_kernel(x_ref, g_ref, o_ref):
    x = x_ref[...]
    ms = jnp.mean(x * x, axis=-1, keepdims=True)
    o_ref[...] = (x * lax.rsqrt(ms + NORM_EPS) * g_ref[...]).astype(o_ref.dtype)


def _rmsnorm(x, g, out_dtype):
    n, d = x.shape
    return pl.pallas_call(
        _rmsnorm_kernel,
        grid=(n // NORM_TM,),
        in_specs=[pl.BlockSpec((NORM_TM, d), lambda i: (i, 0)),
                  pl.BlockSpec((1, d), lambda i: (0, 0))],
        out_specs=pl.BlockSpec((NORM_TM, d), lambda i: (i, 0)),
        out_shape=jax.ShapeDtypeStruct((n, d), out_dtype),
        compiler_params=_cparams("arbitrary"),
        name="rmsnorm",
    )(x, g.reshape(1, d))


def _mm_kernel(*refs, n_extra, epilogue):
    lhs_ref, w_ref = refs[0], refs[1]
    extras = refs[2:2 + n_extra]
    o_ref = refs[2 + n_extra]
    wbf_ref = refs[3 + n_extra]

    @pl.when(pl.program_id(1) == 0)
    def _():
        wbf_ref[...] = w_ref[...].astype(bf16)

    acc = jnp.dot(lhs_ref[...], wbf_ref[...], preferred_element_type=f32)
    o_ref[...] = epilogue(acc, *[e[...] for e in extras]).astype(o_ref.dtype)


def _matmul(lhs, w, col_off, ncols, out_dtype, epilogue, extras=(), extra_specs=(), name="mm"):
    m, k = lhs.shape
    tm, tn = min(MM_TM, m), min(MM_TN, ncols)
    assert m % tm == 0 and ncols % tn == 0 and col_off % tn == 0
    off = col_off // tn
    return pl.pallas_call(
        functools.partial(_mm_kernel, n_extra=len(extras), epilogue=epilogue),
        grid=(ncols // tn, m // tm),
        in_specs=[pl.BlockSpec((tm, k), lambda j, i: (i, 0)),
                  pl.BlockSpec((k, tn), lambda j, i: (0, j + off))] + list(extra_specs),
        out_specs=pl.BlockSpec((tm, tn), lambda j, i: (i, j)),
        out_shape=jax.ShapeDtypeStruct((m, ncols), out_dtype),
        scratch_shapes=[pltpu.VMEM((k, tn), bf16)],
        compiler_params=_cparams("arbitrary", "arbitrary"),
        name=name,
    )(lhs, w, *extras)


def _lru_kernel(u_ref, gg_ref, cw_ref, cb_ref, wr_ref, br_ref, wi_ref, bi_ref, lam_ref,
                o_ref, tail_ref, carry_ref):
    t_len, c = u_ref.shape

    @pl.when(pl.program_id(2) == 0)
    def _():
        tail_ref[...] = jnp.zeros_like(tail_ref)
        carry_ref[...] = jnp.zeros_like(carry_ref)

    u = u_ref[...]
    ext = jnp.concatenate([tail_ref[...], u], axis=0)
    cw = cw_ref[...]
    kw = cw.shape[0]
    uc = ext[SUBLANES:] * cw[kw - 1:kw] + cb_ref[...]
    for s in range(1, kw):
        uc = uc + pltpu.roll(ext, s, 0)[SUBLANES:] * cw[kw - 1 - s:kw - s]
    tail_ref[...] = u[t_len - SUBLANES:]

    ub = uc.astype(bf16)
    r = jax.nn.sigmoid(jnp.dot(ub, wr_ref[0].astype(bf16), preferred_element_type=f32) + br_ref[...])
    i = jax.nn.sigmoid(jnp.dot(ub, wi_ref[0].astype(bf16), preferred_element_type=f32) + bi_ref[...])
    log_a = -LRU_C * r * jax.nn.softplus(-lam_ref[...])
    a = jnp.exp(log_a)
    b = jnp.sqrt(-jnp.tanh(log_a) * (1.0 + a * a)) * (i * uc)

    row = lax.broadcasted_iota(jnp.int32, (t_len, c), 0)
    d = 1
    while d < t_len:
        keep = row >= d
        b = jnp.where(keep, a * pltpu.roll(b, d, 0) + b, b)
        a = jnp.where(keep, a * pltpu.roll(a, d, 0), a)
        d *= 2
    h = b + a * carry_ref[...]
    carry_ref[...] = h[t_len - 1:t_len]
    o_ref[...] = (h * gg_ref[...].astype(f32)).astype(o_ref.dtype)


def _lru_branch(u, gg, conv_w, conv_b, wr, br, wi, bi, lam, bsz, seq):
    n, d = u.shape
    heads, hd, _ = wr.shape
    kw = conv_w.shape[0]
    t_len = min(LRU_T, seq)
    nt = seq // t_len
    row = lambda b, h, t: (b * nt + t, h)
    vec = lambda b, h, t: (0, h)
    return pl.pallas_call(
        _lru_kernel,
        grid=(bsz, heads, nt),
        in_specs=[pl.BlockSpec((t_len, hd), row),
                  pl.BlockSpec((t_len, hd), row),
                  pl.BlockSpec((kw, hd), vec),
                  pl.BlockSpec((1, hd), vec),
                  pl.BlockSpec((1, hd, hd), lambda b, h, t: (h, 0, 0)),
                  pl.BlockSpec((1, hd), vec),
                  pl.BlockSpec((1, hd, hd), lambda b, h, t: (h, 0, 0)),
                  pl.BlockSpec((1, hd), vec),
                  pl.BlockSpec((1, hd), vec)],
        out_specs=pl.BlockSpec((t_len, hd), row),
        out_shape=jax.ShapeDtypeStruct((n, d), bf16),
        scratch_shapes=[pltpu.VMEM((SUBLANES, hd), f32), pltpu.VMEM((1, hd), f32)],
        compiler_params=_cparams("arbitrary", "arbitrary", "arbitrary"),
        name="rg_lru",
    )(u, gg, conv_w, conv_b.reshape(1, d), wr, br.reshape(1, d), wi, bi.reshape(1, d),
      lam.reshape(1, d))


_POOL_HALO = 16


def _pool_kernel(v_ref, o_ref, tail_ref):
    t_len, c = v_ref.shape
    gd = c // len(POOL_WINDOWS)
    t = pl.program_id(1)

    @pl.when(t == 0)
    def _():
        tail_ref[...] = jnp.zeros_like(tail_ref)

    v = v_ref[...]
    ext = jnp.concatenate([tail_ref[...], v], axis=0)
    tail_ref[...] = v[t_len - _POOL_HALO:]
    pos1 = t * t_len + lax.broadcasted_iota(jnp.int32, (t_len, 1), 0) + 1
    for g, w in enumerate(POOL_WINDOWS):
        s = ext[:, g * gd:(g + 1) * gd]
        span = 1
        while span < w:
            s = s + pltpu.roll(s, span, 0)
            span *= 2
        count = jnp.minimum(pos1, w).astype(f32)
        o_ref[:, g * gd:(g + 1) * gd] = (
            s[_POOL_HALO:] / count - v[:, g * gd:(g + 1) * gd]).astype(o_ref.dtype)


def _pool_branch(v, bsz, seq):
    n, c = v.shape
    assert max(POOL_WINDOWS) <= _POOL_HALO
    t_len = min(POOL_T, seq)
    nt = seq // t_len
    return pl.pallas_call(
        _pool_kernel,
        grid=(bsz, nt),
        in_specs=[pl.BlockSpec((t_len, c), lambda b, t: (b * nt + t, 0))],
        out_specs=pl.BlockSpec((t_len, c), lambda b, t: (b * nt + t, 0)),
        out_shape=jax.ShapeDtypeStruct((n, c), bf16),
        scratch_shapes=[pltpu.VMEM((_POOL_HALO, c), f32)],
        compiler_params=_cparams("arbitrary", "arbitrary"),
        name="pool",
    )(v)


def _router_kernel(x_ref, g_ref, wc_ref, bc_ref, h_ref, ids_ref, wts_ref, *, n_groups, per_group):
    x = x_ref[...]
    ms = jnp.mean(x * x, axis=-1, keepdims=True)
    h = x * lax.rsqrt(ms + NORM_EPS) * g_ref[...]
    h_ref[...] = h

    hh = h.astype(bf16)
    hl = (h - hh.astype(f32)).astype(bf16)
    wc = wc_ref[...]
    wh = wc.astype(bf16)
    wl = (wc - wh.astype(f32)).astype(bf16)
    logits = (jnp.dot(hh, wh, preferred_element_type=f32)
              + jnp.dot(hl, wh, preferred_element_type=f32)
              + jnp.dot(hh, wl, preferred_element_type=f32))
    lt = logits.T + bc_ref[...]

    tm = x.shape[0]
    row = lax.broadcasted_iota(jnp.int32, (per_group, tm), 0)

    def softmax_rows(z):
        e = jnp.exp(z - jnp.max(z, axis=0, keepdims=True))
        return e / jnp.sum(e, axis=0, keepdims=True)

    def top1(p):
        pmax = jnp.max(p, axis=0, keepdims=True)
        idx = jnp.min(jnp.where(p == pmax, row, per_group), axis=0, keepdims=True)
        return pmax, idx

    assert n_groups == per_group == SUBLANES
    g_p, g_idx = top1(softmax_rows(lt[0:n_groups]))
    sel = jnp.zeros((per_group, tm), f32)
    for g in range(n_groups):
        lo = n_groups + g * per_group
        sel = jnp.where(g_idx == g, lt[lo:lo + per_group], sel)
    p = softmax_rows(sel)
    p1, i1 = top1(p)
    p2, i2 = top1(jnp.where(row == i1, -1.0, p))
    den = p1 + p2
    w1 = g_p * p1 / den
    w2 = g_p * p2 / den
    id1 = g_idx * per_group + i1
    id2 = g_idx * per_group + i2
    ids_ref[...] = jnp.where(row == 0, id1, jnp.where(row == 1, id2, 0))
    wts_ref[...] = jnp.where(row == 0, w1, jnp.where(row == 1, w2, 0.0))


def _router(x1, g, wg, bg, we, be):
    n, d = x1.shape
    n_groups = wg.shape[1]
    per_group = we.shape[1] // n_groups
    ncat = n_groups + we.shape[1]
    assert ncat <= LANES
    wc = jnp.pad(jnp.concatenate([wg, we], axis=1), ((0, 0), (0, LANES - ncat)))
    bc = jnp.pad(jnp.concatenate([bg, be]), (0, LANES - ncat)).reshape(LANES, 1)
    tm = min(ROUTER_TM, n)
    return pl.pallas_call(
        functools.partial(_router_kernel, n_groups=n_groups, per_group=per_group),
        grid=(n // tm,),
        in_specs=[pl.BlockSpec((tm, d), lambda i: (i, 0)),
                  pl.BlockSpec((1, d), lambda i: (0, 0)),
                  pl.BlockSpec((d, LANES), lambda i: (0, 0)),
                  pl.BlockSpec((LANES, 1), lambda i: (0, 0))],
        out_specs=[pl.BlockSpec((tm, d), lambda i: (i, 0)),
                   pl.BlockSpec((SUBLANES, tm), lambda i: (0, i)),
                   pl.BlockSpec((SUBLANES, tm), lambda i: (0, i))],
        out_shape=[jax.ShapeDtypeStruct((n, d), f32),
                   jax.ShapeDtypeStruct((SUBLANES, n), jnp.int32),
                   jax.ShapeDtypeStruct((SUBLANES, n), f32)],
        compiler_params=_cparams("arbitrary"),
        name="norm_router",
    )(x1, g.reshape(1, d), wc, bc)


def _routing_tables(ids, wts, n_experts, blk):
    n_tok = ids.shape[1]
    n_assign = n_tok * TOP_K
    e_flat = ids[:TOP_K].T.reshape(-1)
    w_flat = wts[:TOP_K].T.reshape(-1)
    tok_flat = jnp.repeat(jnp.arange(n_tok, dtype=jnp.int32), TOP_K)
    order = jnp.argsort(e_flat)
    e_sorted = e_flat[order]
    counts = jnp.bincount(e_flat, length=n_experts).astype(jnp.int32)
    padded = (counts + blk - 1) // blk * blk
    offsets = jnp.cumsum(counts) - counts
    padded_ends = jnp.cumsum(padded)
    padded_offsets = padded_ends - padded
    dest = padded_offsets[e_sorted] + jnp.arange(n_assign, dtype=jnp.int32) - offsets[e_sorted]
    max_rows = n_assign + n_experts * blk
    n_blocks = max_rows // blk
    row_tok = jnp.zeros((max_rows,), jnp.int32).at[dest].set(tok_flat[order])
    row_w = jnp.zeros((max_rows,), f32).at[dest].set(w_flat[order])
    pos = jnp.zeros((n_assign,), jnp.int32).at[order].set(dest)
    block_start = jnp.arange(n_blocks, dtype=jnp.int32) * blk
    block_expert = jnp.minimum(jnp.searchsorted(padded_ends, block_start, side='right'),
                               n_experts - 1).astype(jnp.int32)
    n_used = (padded_ends[-1] // blk).astype(jnp.int32).reshape(1)
    return row_tok, row_w, pos, block_expert, n_used


def _start_row_gather(idx_ref, n_rows, src_hbm, buf, sem, slot):
    def body(r, carry):
        pltpu.make_async_copy(src_hbm.at[pl.ds(idx_ref[0, 0, r], 1)],
                              buf.at[slot, pl.ds(r, 1)], sem.at[slot]).start()
        return carry
    lax.fori_loop(0, n_rows, body, 0)


def _wait_row_gather(n_rows, src_hbm, buf, sem, slot):
    def body(r, carry):
        pltpu.make_async_copy(src_hbm.at[pl.ds(0, 1)],
                              buf.at[slot, pl.ds(r, 1)], sem.at[slot]).wait()
        return carry
    lax.fori_loop(0, n_rows, body, 0)


def _moe_kernel(bexp_ref, nused_ref, tok_cur_ref, tok_nxt_ref, h_hbm, wgu_ref, wdn_ref, roww_ref,
                o_ref, buf, sem):
    del bexp_ref
    b = pl.program_id(0)
    n_used = nused_ref[0]
    blk = buf.shape[1]
    slot = b % 2

    @pl.when(b == 0)
    def _():
        _start_row_gather(tok_cur_ref, blk, h_hbm, buf, sem, 0)

    @pl.when(b + 1 < n_used)
    def _():
        _start_row_gather(tok_nxt_ref, blk, h_hbm, buf, sem, 1 - slot)

    @pl.when(b < n_used)
    def _():
        _wait_row_gather(blk, h_hbm, buf, sem, slot)
        x = buf[slot].astype(bf16)
        gu = jnp.dot(x, wgu_ref[0], preferred_element_type=f32)
        f = gu.shape[1] // 2
        act = (jax.nn.silu(gu[:, :f]) * gu[:, f:]).astype(bf16)
        y = jnp.dot(act, wdn_ref[0], preferred_element_type=f32)
        o_ref[...] = y * roww_ref[...]

    @pl.when(b >= n_used)
    def _():
        o_ref[...] = jnp.zeros_like(o_ref)


def _experts(h, w_gu, w_down, row_tok, row_w, block_expert, n_used, blk):
    n, d = h.shape
    n_exp, _, f2 = w_gu.shape
    max_rows = row_tok.shape[0]
    nb = max_rows // blk
    tok3 = row_tok.reshape(nb, 1, blk)
    grid_spec = pltpu.PrefetchScalarGridSpec(
        num_scalar_prefetch=2,
        grid=(nb,),
        in_specs=[
            pl.BlockSpec((1, 1, blk), lambda b, be, nu: (b, 0, 0), memory_space=pltpu.SMEM),
            pl.BlockSpec((1, 1, blk), lambda b, be, nu: (jnp.minimum(b + 1, nb - 1), 0, 0),
                         memory_space=pltpu.SMEM),
            pl.BlockSpec(memory_space=pl.ANY),
            pl.BlockSpec((1, d, f2), lambda b, be, nu: (be[b], 0, 0)),
            pl.BlockSpec((1, f2 // 2, d), lambda b, be, nu: (be[b], 0, 0)),
            pl.BlockSpec((blk, 1), lambda b, be, nu: (b, 0)),
        ],
        out_specs=pl.BlockSpec((blk, d), lambda b, be, nu: (jnp.minimum(b, nu[0]), 0)),
        scratch_shapes=[pltpu.VMEM((2, blk, d), f32), pltpu.SemaphoreType.DMA((2,))],
    )
    return pl.pallas_call(
        _moe_kernel,
        grid_spec=grid_spec,
        out_shape=jax.ShapeDtypeStruct((max_rows + blk, d), f32),
        compiler_params=_cparams("arbitrary"),
        name="experts",
    )(block_expert, n_used, tok3, tok3, h, w_gu, w_down, row_w.reshape(max_rows, 1))


def _combine_kernel(pos_cur_ref, pos_nxt_ref, x_ref, g_ref, y_hbm, o_ref, buf, sem):
    i = pl.program_id(0)
    n_rows = buf.shape[1]
    tm = x_ref.shape[0]
    slot = i % 2

    @pl.when(i == 0)
    def _():
        _start_row_gather(pos_cur_ref, n_rows, y_hbm, buf, sem, 0)

    @pl.when(i + 1 < pl.num_programs(0))
    def _():
        _start_row_gather(pos_nxt_ref, n_rows, y_hbm, buf, sem, 1 - slot)

    _wait_row_gather(n_rows, y_hbm, buf, sem, slot)
    y = buf[slot, 0:tm] + buf[slot, tm:2 * tm]
    x = x_ref[...] + y
    ms = jnp.mean(x * x, axis=-1, keepdims=True)
    o_ref[...] = x * lax.rsqrt(ms + NORM_EPS) * g_ref[...]


def _combine(x1, g, y_rows, pos):
    n, d = x1.shape
    tm = min(COMBINE_TM, n)
    nt = n // tm
    pos3 = pos.reshape(nt, tm, TOP_K).transpose(0, 2, 1).reshape(nt, 1, TOP_K * tm)
    return pl.pallas_call(
        _combine_kernel,
        grid=(nt,),
        in_specs=[
            pl.BlockSpec((1, 1, TOP_K * tm), lambda i: (i, 0, 0), memory_space=pltpu.SMEM),
            pl.BlockSpec((1, 1, TOP_K * tm), lambda i: (jnp.minimum(i + 1, nt - 1), 0, 0),
                         memory_space=pltpu.SMEM),
            pl.BlockSpec((tm, d), lambda i: (i, 0)),
            pl.BlockSpec((1, d), lambda i: (0, 0)),
            pl.BlockSpec(memory_space=pl.ANY),
        ],
        out_specs=pl.BlockSpec((tm, d), lambda i: (i, 0)),
        out_shape=jax.ShapeDtypeStruct((n, d), f32),
        scratch_shapes=[pltpu.VMEM((2, TOP_K * tm, d), f32), pltpu.SemaphoreType.DMA((2,))],
        compiler_params=_cparams("arbitrary"),
        name="combine_norm",
    )(pos3, pos3, x1, g.reshape(1, d), y_rows)


def _layer(x2, bsz, seq, norm_mix_g, w_in, b_gate, conv_w, conv_b, lru_wr, lru_br, lru_wi, lru_bi,
           lru_lambda, w_lru_out, pool_w, pool_scale, w_out, norm_ffn_g, router_wg, router_bg,
           router_we, router_be, exp_w_gu, exp_w_down):
    n, d = x2.shape
    d_lru = w_lru_out.shape[0]
    n_pool_groups, pool_gd, pool_od = pool_w.shape
    d_pool = n_pool_groups * pool_gd
    col_gate, col_pool, col_merge = d_lru, 2 * d_lru, 2 * d_lru + d_pool
    tn = min(MM_TN, d)
    tm = min(MM_TM, n)

    xn = _rmsnorm(x2, norm_mix_g, bf16)
    u = _matmul(xn, w_in, 0, d_lru, f32, lambda acc: acc, name="in_proj_u")
    gg = _matmul(xn, w_in, col_gate, d_lru, bf16, lambda acc: jax.nn.gelu(acc), name="in_proj_gate")
    v = _matmul(xn, w_in, col_pool, d_pool, f32, lambda acc: acc, name="in_proj_pool")
    mg = _matmul(xn, w_in, col_merge, 2 * d, bf16,
                 lambda acc, b: jax.nn.sigmoid(acc + b),
                 extras=(b_gate.reshape(1, 2 * d),),
                 extra_specs=(pl.BlockSpec((1, tn), lambda j, i: (0, j)),),
                 name="in_proj_merge")

    hg = _lru_branch(u, gg, conv_w, conv_b, lru_wr, lru_br, lru_wi, lru_bi, lru_lambda, bsz, seq)
    pooled = _pool_branch(v, bsz, seq)

    assert pool_od % tn == 0
    per_g = pool_od // tn

    def merge_epilogue(acc, pooled_t, pw, ps, mga, mgb):
        yb = jnp.dot(pooled_t, pw[0].astype(bf16), preferred_element_type=f32) * ps
        return mga.astype(f32) * acc + mgb.astype(f32) * yb

    merged = _matmul(
        hg, w_lru_out, 0, d, bf16, merge_epilogue,
        extras=(pooled, pool_w, pool_scale.reshape(1, d), mg, mg),
        extra_specs=(pl.BlockSpec((tm, pool_gd), lambda j, i: (i, j // per_g)),
                     pl.BlockSpec((1, pool_gd, tn), lambda j, i: (j // per_g, 0, j % per_g)),
                     pl.BlockSpec((1, tn), lambda j, i: (0, j)),
                     pl.BlockSpec((tm, tn), lambda j, i: (i, j)),
                     pl.BlockSpec((tm, tn), lambda j, i: (i, j + d // tn))),
        name="lru_out_merge")

    x1 = _matmul(merged, w_out, 0, d, f32, lambda acc, xr: xr + acc,
                 extras=(x2,), extra_specs=(pl.BlockSpec((tm, tn), lambda j, i: (i, j)),),
                 name="out_proj")

    h2, ids, wts = _router(x1, norm_ffn_g, router_wg, router_bg, router_we, router_be)
    n_experts = exp_w_gu.shape[0]
    row_tok, row_w, pos, block_expert, n_used = _routing_tables(ids, wts, n_experts, EXPERT_BLOCK)
    y_rows = _experts(h2, exp_w_gu.astype(bf16), exp_w_down.astype(bf16), row_tok, row_w,
                      block_expert, n_used, EXPERT_BLOCK)
    return x1, y_rows, pos


def kernel(x, norm_mix_g, w_in, b_gate, conv_w, conv_b, lru_wr, lru_br, lru_wi, lru_bi, lru_lambda,
           w_lru_out, pool_w, pool_scale, w_out, norm_ffn_g, router_wg, router_bg, router_we,
           router_be, exp_w_gu, exp_w_down, norm_final_g):
    bsz, seq, d = x.shape
    depth = w_in.shape[0]
    assert depth == 1
    x2 = x.reshape(bsz * seq, d)
    x1, y_rows, pos = _layer(
        x2, bsz, seq, norm_mix_g[0], w_in[0], b_gate[0], conv_w[0], conv_b[0], lru_wr[0], lru_br[0],
        lru_wi[0], lru_bi[0], lru_lambda[0], w_lru_out[0], pool_w[0], pool_scale[0], w_out[0],
        norm_ffn_g[0], router_wg[0], router_bg[0], router_we[0], router_be[0], exp_w_gu[0],
        exp_w_down[0])
    out = _combine(x1, norm_final_g, y_rows, pos)
    return out.reshape(bsz, seq, d)
```

```python
import functools

import jax
import jax.numpy as jnp
from jax import lax
from jax.experimental import pallas as pl
from jax.experimental.pallas import tpu as pltpu

NORM_EPS = 1e-6
LRU_C = 8.0
POOL_WINDOWS = (2, 4, 8, 16)
TOP_K = 2

VMEM_LIMIT_BYTES = 56 * 1024 * 1024
SUBLANES = 8
LANES = 128

MM_TM = 1024
MM_TN = 512
NORM_TM = 512
LRU_TM = 1024
LRU_PARTS = 4
POOL_T = 512
ROUTER_TM = 256
EXPERT_BLOCK = 256
COMBINE_TM = 128

f32 = jnp.float32
bf16 = jnp.bfloat16


def _cparams(*sem):
    return pltpu.CompilerParams(dimension_semantics=sem, vmem_limit_bytes=VMEM_LIMIT_BYTES)


def _rmsnorm_kernel(x_ref, g_ref, o_ref):
    x = x_ref[...]
    ms = jnp.mean(x * x, axis=-1, keepdims=True)
    o_ref[...] = (x * lax.rsqrt(ms + NORM_EPS) * g_ref[...]).astype(o_ref.dtype)


def _rmsnorm(x, g, out_dtype):
    n, d = x.shape
    return pl.pallas_call(
        _rmsnorm_kernel,
        grid=(n // NORM_TM,),
        in_specs=[pl.BlockSpec((NORM_TM, d), lambda i: (i, 0)),
                  pl.BlockSpec((1, d), lambda i: (0, 0))],
        out_specs=pl.BlockSpec((NORM_TM, d), lambda i: (i, 0)),
        out_shape=jax.ShapeDtypeStruct((n, d), out_dtype),
        compiler_params=_cparams("arbitrary"),
        name="rmsnorm",
    )(x, g.reshape(1, d))


def _mm_kernel(*refs, n_extra, epilogue):
    lhs_ref, w_ref = refs[0], refs[1]
    extras = refs[2:2 + n_extra]
    o_ref = refs[2 + n_extra]
    wbf_ref = refs[3 + n_extra]

    @pl.when(pl.program_id(1) == 0)
    def _():
        wbf_ref[...] = w_ref[...].astype(bf16)

    acc = jnp.dot(lhs_ref[...], wbf_ref[...], preferred_element_type=f32)
    o_ref[...] = epilogue(acc, *[e[...] for e in extras]).astype(o_ref.dtype)


def _matmul(lhs, w, col_off, ncols, out_dtype, epilogue, extras=(), extra_specs=(), name="mm"):
    m, k = lhs.shape
    tm, tn = min(MM_TM, m), min(MM_TN, ncols)
    assert m % tm == 0 and ncols % tn == 0 and col_off % tn == 0
    off = col_off // tn
    return pl.pallas_call(
        functools.partial(_mm_kernel, n_extra=len(extras), epilogue=epilogue),
        grid=(ncols // tn, m // tm),
        in_specs=[pl.BlockSpec((tm, k), lambda j, i: (i, 0)),
                  pl.BlockSpec((k, tn), lambda j, i: (0, j + off))] + list(extra_specs),
        out_specs=pl.BlockSpec((tm, tn), lambda j, i: (i, j)),
        out_shape=jax.ShapeDtypeStruct((m, ncols), out_dtype),
        scratch_shapes=[pltpu.VMEM((k, tn), bf16)],
        compiler_params=_cparams("arbitrary", "arbitrary"),
        name=name,
    )(lhs, w, *extras)


def _lru_part(u, gate, tail, carry, cw, cb, wr, br, wi, bi, neg_c_sp):
    rows, c = u.shape
    kw = cw.shape[0]
    ext = jnp.concatenate([tail, u], axis=0)
    uc = ext[SUBLANES:] * cw[kw - 1:kw] + cb
    for s in range(1, kw):
        uc = uc + pltpu.roll(ext, s, 0)[SUBLANES:] * cw[kw - 1 - s:kw - s]

    ub = uc.astype(bf16)
    r = jax.nn.sigmoid(jnp.dot(ub, wr, preferred_element_type=f32) + br)
    i = jax.nn.sigmoid(jnp.dot(ub, wi, preferred_element_type=f32) + bi)
    log_a = r * neg_c_sp
    a = jnp.exp(log_a)
    b = jnp.sqrt(-jnp.tanh(log_a) * (1.0 + a * a)) * (i * uc)

    groups = rows // SUBLANES
    ae = jnp.concatenate([jnp.ones((SUBLANES, c), f32), a], axis=0)
    be = jnp.concatenate([jnp.zeros((SUBLANES, c), f32), b], axis=0)
    d = 1
    while d < SUBLANES:
        be = ae * pltpu.roll(be, d, 0) + be
        ae = ae * pltpu.roll(ae, d, 0)
        d *= 2
    a3 = ae[SUBLANES:].reshape(groups, SUBLANES, c)
    b3 = be[SUBLANES:].reshape(groups, SUBLANES, c)
    m = 1
    while m < groups:
        b3 = jnp.concatenate([b3[:m], a3[m:] * b3[:-m] + b3[m:]], axis=0)
        a3 = jnp.concatenate([a3[:m], a3[m:] * a3[:-m]], axis=0)
        m *= 2
    h = b3.reshape(rows, c) + a3.reshape(rows, c) * carry
    return h * jax.nn.gelu(gate), u[rows - SUBLANES:], h[rows - 1:rows]


def _inproj_lru_kernel(xn_ref, wu_ref, wg_ref, cw_ref, cb_ref, wr_ref, br_ref, wi_ref, bi_ref,
                       lam_ref, o_ref, wbf_ref, tail_ref, carry_ref, *, tiles_per_seq, parts):
    hd = wu_ref.shape[1]
    i = pl.program_id(1)

    @pl.when(i == 0)
    def _():
        wbf_ref[:, :hd] = wu_ref[...].astype(bf16)
        wbf_ref[:, hd:] = wg_ref[...].astype(bf16)

    @pl.when(i % tiles_per_seq == 0)
    def _():
        tail_ref[...] = jnp.zeros_like(tail_ref)
        carry_ref[...] = jnp.zeros_like(carry_ref)

    cw, cb = cw_ref[...], cb_ref[...]
    wr, wi = wr_ref[0].astype(bf16), wi_ref[0].astype(bf16)
    br, bi = br_ref[...], bi_ref[...]
    neg_c_sp = -LRU_C * jax.nn.softplus(-lam_ref[...])
    tail, carry = tail_ref[...], carry_ref[...]
    rows = xn_ref.shape[0] // parts
    for p in range(parts):
        acc = jnp.dot(xn_ref[p * rows:(p + 1) * rows, :], wbf_ref[...], preferred_element_type=f32)
        hg, tail, carry = _lru_part(acc[:, :hd], acc[:, hd:], tail, carry, cw, cb, wr, br, wi, bi,
                                    neg_c_sp)
        o_ref[p * rows:(p + 1) * rows, :] = hg.astype(o_ref.dtype)
    tail_ref[...] = tail
    carry_ref[...] = carry


def _inproj_lru(xn, w_in, col_gate, conv_w, conv_b, wr, br, wi, bi, lam, seq):
    n, k = xn.shape
    heads, hd, _ = wr.shape
    d_lru = heads * hd
    kw = conv_w.shape[0]
    tm = min(LRU_TM, seq)
    assert seq % tm == 0 and col_gate % hd == 0 and (tm // LRU_PARTS) % SUBLANES == 0
    gate_off = col_gate // hd
    vec = lambda h, i: (0, h)
    return pl.pallas_call(
        functools.partial(_inproj_lru_kernel, tiles_per_seq=seq // tm, parts=LRU_PARTS),
        grid=(heads, n // tm),
        in_specs=[pl.BlockSpec((tm, k), lambda h, i: (i, 0)),
                  pl.BlockSpec((k, hd), lambda h, i: (0, h)),
                  pl.BlockSpec((k, hd), lambda h, i: (0, gate_off + h)),
                  pl.BlockSpec((kw, hd), vec),
                  pl.BlockSpec((1, hd), vec),
                  pl.BlockSpec((1, hd, hd), lambda h, i: (h, 0, 0)),
                  pl.BlockSpec((1, hd), vec),
                  pl.BlockSpec((1, hd, hd), lambda h, i: (h, 0, 0)),
                  pl.BlockSpec((1, hd), vec),
                  pl.BlockSpec((1, hd), vec)],
        out_specs=pl.BlockSpec((tm, hd), lambda h, i: (i, h)),
        out_shape=jax.ShapeDtypeStruct((n, d_lru), bf16),
        scratch_shapes=[pltpu.VMEM((k, 2 * hd), bf16), pltpu.VMEM((SUBLANES, hd), f32),
                        pltpu.VMEM((1, hd), f32)],
        compiler_params=_cparams("arbitrary", "arbitrary"),
        name="in_proj_rg_lru",
    )(xn, w_in, w_in, conv_w, conv_b.reshape(1, d_lru), wr, br.reshape(1, d_lru), wi,
      bi.reshape(1, d_lru), lam.reshape(1, d_lru))


_POOL_HALO = 16


def _pool_kernel(v_ref, o_ref, tail_ref):
    t_len, c = v_ref.shape
    gd = c // len(POOL_WINDOWS)
    t = pl.program_id(1)

    @pl.when(t == 0)
    def _():
        tail_ref[...] = jnp.zeros_like(tail_ref)

    v = v_ref[...]
    ext = jnp.concatenate([tail_ref[...], v], axis=0)
    tail_ref[...] = v[t_len - _POOL_HALO:]
    pos1 = t * t_len + lax.broadcasted_iota(jnp.int32, (t_len, 1), 0) + 1
    for g, w in enumerate(POOL_WINDOWS):
        s = ext[:, g * gd:(g + 1) * gd]
        span = 1
        while span < w:
            s = s + pltpu.roll(s, span, 0)
            span *= 2
        count = jnp.minimum(pos1, w).astype(f32)
        o_ref[:, g * gd:(g + 1) * gd] = (
            s[_POOL_HALO:] / count - v[:, g * gd:(g + 1) * gd]).astype(o_ref.dtype)


def _pool_branch(v, bsz, seq):
    n, c = v.shape
    assert max(POOL_WINDOWS) <= _POOL_HALO
    t_len = min(POOL_T, seq)
    nt = seq // t_len
    return pl.pallas_call(
        _pool_kernel,
        grid=(bsz, nt),
        in_specs=[pl.BlockSpec((t_len, c), lambda b, t: (b * nt + t, 0))],
        out_specs=pl.BlockSpec((t_len, c), lambda b, t: (b * nt + t, 0)),
        out_shape=jax.ShapeDtypeStruct((n, c), bf16),
        scratch_shapes=[pltpu.VMEM((_POOL_HALO, c), f32)],
        compiler_params=_cparams("arbitrary", "arbitrary"),
        name="pool",
    )(v)


_HI16 = 0xFFFF0000


def _pack_bf16_pairs(xb):
    half = xb.shape[1] // 2
    bits = lax.bitcast_convert_type(xb.astype(f32), jnp.uint32)
    return (bits[:, :half] >> 16) | (bits[:, half:] & jnp.uint32(_HI16))


def _unpack_bf16_pairs(p):
    lo = lax.bitcast_convert_type(p << 16, f32)
    hi = lax.bitcast_convert_type(p & jnp.uint32(_HI16), f32)
    return jnp.concatenate([lo, hi], axis=1)


def _router_kernel(x_ref, g_ref, wc_ref, bc_ref, h_ref, ids_ref, wts_ref, *, n_groups, per_group):
    x = x_ref[...]
    ms = jnp.mean(x * x, axis=-1, keepdims=True)
    h = x * lax.rsqrt(ms + NORM_EPS) * g_ref[...]

    hh = h.astype(bf16)
    h_ref[...] = _pack_bf16_pairs(hh)
    hl = (h - hh.astype(f32)).astype(bf16)
    wc = wc_ref[...]
    wh = wc.astype(bf16)
    wl = (wc - wh.astype(f32)).astype(bf16)
    logits = (jnp.dot(hh, wh, preferred_element_type=f32)
              + jnp.dot(hl, wh, preferred_element_type=f32)
              + jnp.dot(hh, wl, preferred_element_type=f32))
    lt = logits.T + bc_ref[...]

    tm = x.shape[0]
    row = lax.broadcasted_iota(jnp.int32, (per_group, tm), 0)

    def softmax_rows(z):
        e = jnp.exp(z - jnp.max(z, axis=0, keepdims=True))
        return e / jnp.sum(e, axis=0, keepdims=True)

    def top1(p):
        pmax = jnp.max(p, axis=0, keepdims=True)
        idx = jnp.min(jnp.where(p == pmax, row, per_group), axis=0, keepdims=True)
        return pmax, idx

    assert n_groups == per_group == SUBLANES
    g_p, g_idx = top1(softmax_rows(lt[0:n_groups]))
    sel = jnp.zeros((per_group, tm), f32)
    for g in range(n_groups):
        lo = n_groups + g * per_group
        sel = jnp.where(g_idx == g, lt[lo:lo + per_group], sel)
    p = softmax_rows(sel)
    p1, i1 = top1(p)
    p2, i2 = top1(jnp.where(row == i1, -1.0, p))
    den = p1 + p2
    w1 = g_p * p1 / den
    w2 = g_p * p2 / den
    id1 = g_idx * per_group + i1
    id2 = g_idx * per_group + i2
    ids_ref[...] = jnp.where(row == 0, id1, jnp.where(row == 1, id2, 0))
    wts_ref[...] = jnp.where(row == 0, w1, jnp.where(row == 1, w2, 0.0))


def _router(x1, g, wg, bg, we, be):
    n, d = x1.shape
    n_groups = wg.shape[1]
    per_group = we.shape[1] // n_groups
    ncat = n_groups + we.shape[1]
    assert ncat <= LANES
    wc = jnp.pad(jnp.concatenate([wg, we], axis=1), ((0, 0), (0, LANES - ncat)))
    bc = jnp.pad(jnp.concatenate([bg, be]), (0, LANES - ncat)).reshape(LANES, 1)
    tm = min(ROUTER_TM, n)
    return pl.pallas_call(
        functools.partial(_router_kernel, n_groups=n_groups, per_group=per_group),
        grid=(n // tm,),
        in_specs=[pl.BlockSpec((tm, d), lambda i: (i, 0)),
                  pl.BlockSpec((1, d), lambda i: (0, 0)),
                  pl.BlockSpec((d, LANES), lambda i: (0, 0)),
                  pl.BlockSpec((LANES, 1), lambda i: (0, 0))],
        out_specs=[pl.BlockSpec((tm, d // 2), lambda i: (i, 0)),
                   pl.BlockSpec((SUBLANES, tm), lambda i: (0, i)),
                   pl.BlockSpec((SUBLANES, tm), lambda i: (0, i))],
        out_shape=[jax.ShapeDtypeStruct((n, d // 2), jnp.uint32),
                   jax.ShapeDtypeStruct((SUBLANES, n), jnp.int32),
                   jax.ShapeDtypeStruct((SUBLANES, n), f32)],
        compiler_params=_cparams("arbitrary"),
        name="norm_router",
    )(x1, g.reshape(1, d), wc, bc)


def _routing_tables(ids, wts, n_experts, blk):
    n_tok = ids.shape[1]
    n_assign = n_tok * TOP_K
    e_flat = ids[:TOP_K].T.reshape(-1)
    w_flat = wts[:TOP_K].T.reshape(-1)
    order = jnp.argsort(e_flat).astype(jnp.int32)
    inv = jnp.argsort(order).astype(jnp.int32)
    e_sorted = e_flat[order]
    ends = jnp.searchsorted(e_sorted, jnp.arange(n_experts, dtype=jnp.int32),
                            side='right').astype(jnp.int32)
    offsets = jnp.concatenate([jnp.zeros((1,), jnp.int32), ends[:-1]])
    counts = ends - offsets
    padded = (counts + blk - 1) // blk * blk
    padded_ends = jnp.cumsum(padded)
    padded_offsets = padded_ends - padded
    pos = padded_offsets[e_flat] + inv - offsets[e_flat]
    max_rows = n_assign + n_experts * blk
    n_blocks = max_rows // blk
    block_start = jnp.arange(n_blocks, dtype=jnp.int32) * blk
    block_expert = jnp.minimum(jnp.searchsorted(padded_ends, block_start, side='right'),
                               n_experts - 1).astype(jnp.int32)
    row_e = jnp.repeat(block_expert, blk)
    rank = jnp.arange(max_rows, dtype=jnp.int32) - padded_offsets[row_e]
    valid = rank < counts[row_e]
    row_a = order[jnp.clip(offsets[row_e] + rank, 0, n_assign - 1)]
    row_tok = jnp.where(valid, row_a // TOP_K, 0)
    row_w = jnp.where(valid, w_flat[row_a], 0.0)
    n_used = (padded_ends[-1] // blk).astype(jnp.int32).reshape(1)
    return row_tok, row_w, pos, block_expert, n_used


_GATHER_UNROLL = 8


def _start_row_gather(idx_ref, n_rows, src_hbm, buf, sem, slot):
    def body(g, carry):
        for k in range(_GATHER_UNROLL):
            r = g * _GATHER_UNROLL + k
            pltpu.make_async_copy(src_hbm.at[pl.ds(idx_ref[0, 0, r], 1)],
                                  buf.at[slot, pl.ds(r, 1)], sem.at[slot]).start()
        return carry
    lax.fori_loop(0, n_rows // _GATHER_UNROLL, body, 0)


def _wait_row_gather(n_rows, src_hbm, buf, sem, slot):
    del n_rows
    pltpu.make_async_copy(src_hbm.at[pl.ds(0, buf.shape[1])], buf.at[slot], sem.at[slot]).wait()


def _moe_kernel(bexp_ref, nused_ref, tok_cur_ref, tok_nxt_ref, h_hbm, wgu_ref, wdn_ref, roww_ref,
                o_ref, buf, sem):
    del bexp_ref
    b = pl.program_id(0)
    n_used = nused_ref[0]
    blk = buf.shape[1]
    slot = b % 2

    @pl.when(b == 0)
    def _():
        _start_row_gather(tok_cur_ref, blk, h_hbm, buf, sem, 0)

    @pl.when(b + 1 < n_used)
    def _():
        _start_row_gather(tok_nxt_ref, blk, h_hbm, buf, sem, 1 - slot)

    @pl.when(b < n_used)
    def _():
        _wait_row_gather(blk, h_hbm, buf, sem, slot)
        x = _unpack_bf16_pairs(buf[slot]).astype(bf16)
        gu = jnp.dot(x, wgu_ref[0], preferred_element_type=f32)
        f = gu.shape[1] // 2
        act = (jax.nn.silu(gu[:, :f]) * gu[:, f:]).astype(bf16)
        y = jnp.dot(act, wdn_ref[0], preferred_element_type=f32)
        o_ref[...] = y * roww_ref[...]

    @pl.when(b >= n_used)
    def _():
        o_ref[...] = jnp.zeros_like(o_ref)


def _experts(h_packed, w_gu, w_down, row_tok, row_w, block_expert, n_used, blk):
    half = h_packed.shape[1]
    _, d, f2 = w_gu.shape
    assert d == 2 * half
    max_rows = row_tok.shape[0]
    nb = max_rows // blk
    tok3 = row_tok.reshape(nb, 1, blk)
    grid_spec = pltpu.PrefetchScalarGridSpec(
        num_scalar_prefetch=2,
        grid=(nb,),
        in_specs=[
            pl.BlockSpec((1, 1, blk), lambda b, be, nu: (b, 0, 0), memory_space=pltpu.SMEM),
            pl.BlockSpec((1, 1, blk), lambda b, be, nu: (jnp.minimum(b + 1, nb - 1), 0, 0),
                         memory_space=pltpu.SMEM),
            pl.BlockSpec(memory_space=pl.ANY),
            pl.BlockSpec((1, d, f2), lambda b, be, nu: (be[b], 0, 0)),
            pl.BlockSpec((1, f2 // 2, d), lambda b, be, nu: (be[b], 0, 0)),
            pl.BlockSpec((blk, 1), lambda b, be, nu: (b, 0)),
        ],
        out_specs=pl.BlockSpec((blk, d), lambda b, be, nu: (jnp.minimum(b, nu[0]), 0)),
        scratch_shapes=[pltpu.VMEM((2, blk, half), jnp.uint32), pltpu.SemaphoreType.DMA((2,))],
    )
    return pl.pallas_call(
        _moe_kernel,
        grid_spec=grid_spec,
        out_shape=jax.ShapeDtypeStruct((max_rows, d), f32),
        compiler_params=_cparams("arbitrary"),
        name="experts",
    )(block_expert, n_used, tok3, tok3, h_packed, w_gu, w_down, row_w.reshape(max_rows, 1))


def _combine_kernel(pos_cur_ref, pos_nxt_ref, x_ref, g_ref, y_hbm, o_ref, buf, sem):
    i = pl.program_id(0)
    n_rows = buf.shape[1]
    tm = x_ref.shape[0]
    slot = i % 2

    @pl.when(i == 0)
    def _():
        _start_row_gather(pos_cur_ref, n_rows, y_hbm, buf, sem, 0)

    @pl.when(i + 1 < pl.num_programs(0))
    def _():
        _start_row_gather(pos_nxt_ref, n_rows, y_hbm, buf, sem, 1 - slot)

    _wait_row_gather(n_rows, y_hbm, buf, sem, slot)
    y = buf[slot, 0:tm] + buf[slot, tm:2 * tm]
    x = x_ref[...] + y
    ms = jnp.mean(x * x, axis=-1, keepdims=True)
    o_ref[...] = x * lax.rsqrt(ms + NORM_EPS) * g_ref[...]


def _combine(x1, g, y_rows, pos):
    n, d = x1.shape
    tm = min(COMBINE_TM, n)
    nt = n // tm
    pos3 = pos.reshape(nt, tm, TOP_K).transpose(0, 2, 1).reshape(nt, 1, TOP_K * tm)
    return pl.pallas_call(
        _combine_kernel,
        grid=(nt,),
        in_specs=[
            pl.BlockSpec((1, 1, TOP_K * tm), lambda i: (i, 0, 0), memory_space=pltpu.SMEM),
            pl.BlockSpec((1, 1, TOP_K * tm), lambda i: (jnp.minimum(i + 1, nt - 1), 0, 0),
                         memory_space=pltpu.SMEM),
            pl.BlockSpec((tm, d), lambda i: (i, 0)),
            pl.BlockSpec((1, d), lambda i: (0, 0)),
            pl.BlockSpec(memory_space=pl.ANY),
        ],
        out_specs=pl.BlockSpec((tm, d), lambda i: (i, 0)),
        out_shape=jax.ShapeDtypeStruct((n, d), f32),
        scratch_shapes=[pltpu.VMEM((2, TOP_K * tm, d), f32), pltpu.SemaphoreType.DMA((2,))],
        compiler_params=_cparams("arbitrary"),
        name="combine_norm",
    )(pos3, pos3, x1, g.reshape(1, d), y_rows)


def _layer(x2, bsz, seq, norm_mix_g, w_in, b_gate, conv_w, conv_b, lru_wr, lru_br, lru_wi, lru_bi,
           lru_lambda, w_lru_out, pool_w, pool_scale, w_out, norm_ffn_g, router_wg, router_bg,
           router_we, router_be, exp_w_gu, exp_w_down):
    n, d = x2.shape
    d_lru = w_lru_out.shape[0]
    n_pool_groups, pool_gd, pool_od = pool_w.shape
    d_pool = n_pool_groups * pool_gd
    col_gate, col_pool, col_merge = d_lru, 2 * d_lru, 2 * d_lru + d_pool
    tn = min(MM_TN, d)
    tm = min(MM_TM, n)

    xn = _rmsnorm(x2, norm_mix_g, bf16)
    hg = _inproj_lru(xn, w_in, col_gate, conv_w, conv_b, lru_wr, lru_br, lru_wi, lru_bi, lru_lambda,
                     seq)
    v =_matmul(xn, w_in, col_pool, d_pool, f32, lambda acc: acc, name="in_proj_pool")
    mg = _matmul(xn, w_in, col_merge, 2 * d, bf16,
                 lambda acc, b: jax.nn.sigmoid(acc + b),
                 extras=(b_gate.reshape(1, 2 * d),),
                 extra_specs=(pl.BlockSpec((1, tn), lambda j, i: (0, j)),),
                 name="in_proj_merge")

    pooled = _pool_branch(v, bsz, seq)

    assert pool_od % tn == 0
    per_g = pool_od // tn

    def merge_epilogue(acc, pooled_t, pw, ps, mga, mgb):
        yb = jnp.dot(pooled_t, pw[0].astype(bf16), preferred_element_type=f32) * ps
        return mga.astype(f32) * acc + mgb.astype(f32) * yb

    merged = _matmul(
        hg, w_lru_out, 0, d, bf16, merge_epilogue,
        extras=(pooled, pool_w, pool_scale.reshape(1, d), mg, mg),
        extra_specs=(pl.BlockSpec((tm, pool_gd), lambda j, i: (i, j // per_g)),
                     pl.BlockSpec((1, pool_gd, tn), lambda j, i: (j // per_g, 0, j % per_g)),
                     pl.BlockSpec((1, tn), lambda j, i: (0, j)),
                     pl.BlockSpec((tm, tn), lambda j, i: (i, j)),
                     pl.BlockSpec((tm, tn), lambda j, i: (i, j + d // tn))),
        name="lru_out_merge")

    x1 = _matmul(merged, w_out, 0, d, f32, lambda acc, xr: xr + acc,
                 extras=(x2,), extra_specs=(pl.BlockSpec((tm, tn), lambda j, i: (i, j)),),
                 name="out_proj")

    h2, ids, wts = _router(x1, norm_ffn_g, router_wg, router_bg, router_we, router_be)
    n_experts = exp_w_gu.shape[0]
    row_tok, row_w, pos, block_expert, n_used = _routing_tables(ids, wts, n_experts, EXPERT_BLOCK)
    y_rows = _experts(h2, exp_w_gu.astype(bf16), exp_w_down.astype(bf16), row_tok, row_w,
                      block_expert, n_used, EXPERT_BLOCK)
    return x1, y_rows, pos


def kernel(x, norm_mix_g, w_in, b_gate, conv_w, conv_b, lru_wr, lru_br, lru_wi, lru_bi, lru_lambda,
           w_lru_out, pool_w, pool_scale, w_out, norm_ffn_g, router_wg, router_bg, router_we,
           router_be, exp_w_gu, exp_w_down, norm_final_g):
    bsz, seq, d = x.shape
    depth = w_in.shape[0]
    assert depth == 1
    x2 = x.reshape(bsz * seq, d)
    x1, y_rows, pos = _layer(
        x2, bsz, seq, norm_mix_g[0], w_in[0], b_gate[0], conv_w[0], conv_b[0], lru_wr[0], lru_br[0],
        lru_wi[0], lru_bi[0], lru_lambda[0], w_lru_out[0], pool_w[0], pool_scale[0], w_out[0],
        norm_ffn_g[0], router_wg[0], router_bg[0], router_we[0], router_be[0], exp_w_gu[0],
        exp_w_down[0])
    out = _combine(x1, norm_final_g, y_rows, pos)
    return out.reshape(bsz, seq, d)
```

```python
import functools

import jax
import jax.numpy as jnp
from jax import lax
from jax.experimental import pallas as pl
from jax.experimental.pallas import tpu as pltpu

NORM_EPS = 1e-6
LRU_C = 8.0
POOL_WINDOWS = (2, 4, 8, 16)
TOP_K = 2

VMEM_LIMIT_BYTES = 56 * 1024 * 1024
SUBLANES = 8
LANES = 128

MM_TM = 1024
MM_TN = 512
NORM_TM = 512
LRU_TM = 1024
LRU_PARTS = 4
POOL_T = 512
ROUTER_TM = 256
EXPERT_BLOCK = 256
DISPATCH_TM = 256
EXPERTS_VMEM_LIMIT_BYTES = 60 * 1024 * 1024
COMBINE_TM = 128

f32 = jnp.float32
bf16 = jnp.bfloat16


def _cparams(*sem):
    return pltpu.CompilerParams(dimension_semantics=sem, vmem_limit_bytes=VMEM_LIMIT_BYTES)


def _rmsnorm_kernel(x_ref, g_ref, o_ref):
    x = x_ref[...]
    ms = jnp.mean(x * x, axis=-1, keepdims=True)
    o_ref[...] = (x * lax.rsqrt(ms + NORM_EPS) * g_ref[...]).astype(o_ref.dtype)


def _rmsnorm(x, g, out_dtype):
    n, d = x.shape
    return pl.pallas_call(
        _rmsnorm_kernel,
        grid=(n // NORM_TM,),
        in_specs=[pl.BlockSpec((NORM_TM, d), lambda i: (i, 0)),
                  pl.BlockSpec((1, d), lambda i: (0, 0))],
        out_specs=pl.BlockSpec((NORM_TM, d), lambda i: (i, 0)),
        out_shape=jax.ShapeDtypeStruct((n, d), out_dtype),
        compiler_params=_cparams("arbitrary"),
        name="rmsnorm",
    )(x, g.reshape(1, d))


def _mm_kernel(*refs, n_extra, epilogue):
    lhs_ref, w_ref = refs[0], refs[1]
    extras = refs[2:2 + n_extra]
    o_ref = refs[2 + n_extra]
    wbf_ref = refs[3 + n_extra]

    @pl.when(pl.program_id(1) == 0)
    def _():
        wbf_ref[...] = w_ref[...].astype(bf16)

    acc = jnp.dot(lhs_ref[...], wbf_ref[...], preferred_element_type=f32)
    o_ref[...] = epilogue(acc, *[e[...] for e in extras]).astype(o_ref.dtype)


def _matmul(lhs, w, col_off, ncols, out_dtype, epilogue, extras=(), extra_specs=(), name="mm"):
    m, k = lhs.shape
    tm, tn = min(MM_TM, m), min(MM_TN, ncols)
    assert m % tm == 0 and ncols % tn == 0 and col_off % tn == 0
    off = col_off // tn
    return pl.pallas_call(
        functools.partial(_mm_kernel, n_extra=len(extras), epilogue=epilogue),
        grid=(ncols // tn, m // tm),
        in_specs=[pl.BlockSpec((tm, k), lambda j, i: (i, 0)),
                  pl.BlockSpec((k, tn), lambda j, i: (0, j + off))] + list(extra_specs),
        out_specs=pl.BlockSpec((tm, tn), lambda j, i: (i, j)),
        out_shape=jax.ShapeDtypeStruct((m, ncols), out_dtype),
        scratch_shapes=[pltpu.VMEM((k, tn), bf16)],
        compiler_params=_cparams("arbitrary", "arbitrary"),
        name=name,
    )(lhs, w, *extras)


def _lru_part(u, gate, tail, carry, cw, cb, wr, br, wi, bi, neg_c_sp):
    rows, c = u.shape
    kw = cw.shape[0]
    ext = jnp.concatenate([tail, u], axis=0)
    uc = ext[SUBLANES:] * cw[kw - 1:kw] + cb
    for s in range(1, kw):
        uc = uc + pltpu.roll(ext, s, 0)[SUBLANES:] * cw[kw - 1 - s:kw - s]

    ub = uc.astype(bf16)
    r = jax.nn.sigmoid(jnp.dot(ub, wr, preferred_element_type=f32) + br)
    i = jax.nn.sigmoid(jnp.dot(ub, wi, preferred_element_type=f32) + bi)
    log_a = r * neg_c_sp
    a = jnp.exp(log_a)
    b = jnp.sqrt(-jnp.tanh(log_a) * (1.0 + a * a)) * (i * uc)

    groups = rows // SUBLANES
    ae = jnp.concatenate([jnp.ones((SUBLANES, c), f32), a], axis=0)
    be = jnp.concatenate([jnp.zeros((SUBLANES, c), f32), b], axis=0)
    d = 1
    while d < SUBLANES:
        be = ae * pltpu.roll(be, d, 0) + be
        ae = ae * pltpu.roll(ae, d, 0)
        d *= 2
    a3 = ae[SUBLANES:].reshape(groups, SUBLANES, c)
    b3 = be[SUBLANES:].reshape(groups, SUBLANES, c)
    m = 1
    while m < groups:
        b3 = jnp.concatenate([b3[:m], a3[m:] * b3[:-m] + b3[m:]], axis=0)
        a3 = jnp.concatenate([a3[:m], a3[m:] * a3[:-m]], axis=0)
        m *= 2
    h = b3.reshape(rows, c) + a3.reshape(rows, c) * carry
    return h * jax.nn.gelu(gate), u[rows - SUBLANES:], h[rows - 1:rows]


def _inproj_lru_kernel(xn_ref, wu_ref, wg_ref, cw_ref, cb_ref, wr_ref, br_ref, wi_ref, bi_ref,
                       lam_ref, o_ref, wbf_ref, tail_ref, carry_ref, *, tiles_per_seq, parts):
    hd = wu_ref.shape[1]
    i = pl.program_id(1)

    @pl.when(i == 0)
    def _():
        wbf_ref[:, :hd] = wu_ref[...].astype(bf16)
        wbf_ref[:, hd:] = wg_ref[...].astype(bf16)

    @pl.when(i % tiles_per_seq == 0)
    def _():
        tail_ref[...] = jnp.zeros_like(tail_ref)
        carry_ref[...] = jnp.zeros_like(carry_ref)

    cw, cb = cw_ref[...], cb_ref[...]
    wr, wi = wr_ref[0].astype(bf16), wi_ref[0].astype(bf16)
    br, bi = br_ref[...], bi_ref[...]
    neg_c_sp = -LRU_C * jax.nn.softplus(-lam_ref[...])
    tail, carry = tail_ref[...], carry_ref[...]
    rows = xn_ref.shape[0] // parts
    for p in range(parts):
        acc = jnp.dot(xn_ref[p * rows:(p + 1) * rows, :], wbf_ref[...], preferred_element_type=f32)
        hg, tail, carry = _lru_part(acc[:, :hd], acc[:, hd:], tail, carry, cw, cb, wr, br, wi, bi,
                                    neg_c_sp)
        o_ref[p * rows:(p + 1) * rows, :] = hg.astype(o_ref.dtype)
    tail_ref[...] = tail
    carry_ref[...] = carry


def _inproj_lru(xn, w_in, col_gate, conv_w, conv_b, wr, br, wi, bi, lam, seq):
    n, k = xn.shape
    heads, hd, _ = wr.shape
    d_lru = heads * hd
    kw = conv_w.shape[0]
    tm = min(LRU_TM, seq)
    assert seq % tm == 0 and col_gate % hd == 0 and (tm // LRU_PARTS) % SUBLANES == 0
    gate_off = col_gate // hd
    vec = lambda h, i: (0, h)
    return pl.pallas_call(
        functools.partial(_inproj_lru_kernel, tiles_per_seq=seq // tm, parts=LRU_PARTS),
        grid=(heads, n // tm),
        in_specs=[pl.BlockSpec((tm, k), lambda h, i: (i, 0)),
                  pl.BlockSpec((k, hd), lambda h, i: (0, h)),
                  pl.BlockSpec((k, hd), lambda h, i: (0, gate_off + h)),
                  pl.BlockSpec((kw, hd), vec),
                  pl.BlockSpec((1, hd), vec),
                  pl.BlockSpec((1, hd, hd), lambda h, i: (h, 0, 0)),
                  pl.BlockSpec((1, hd), vec),
                  pl.BlockSpec((1, hd, hd), lambda h, i: (h, 0, 0)),
                  pl.BlockSpec((1, hd), vec),
                  pl.BlockSpec((1, hd), vec)],
        out_specs=pl.BlockSpec((tm, hd), lambda h, i: (i, h)),
        out_shape=jax.ShapeDtypeStruct((n, d_lru), bf16),
        scratch_shapes=[pltpu.VMEM((k, 2 * hd), bf16), pltpu.VMEM((SUBLANES, hd), f32),
                        pltpu.VMEM((1, hd), f32)],
        compiler_params=_cparams("arbitrary", "arbitrary"),
        name="in_proj_rg_lru",
    )(xn, w_in, w_in, conv_w, conv_b.reshape(1, d_lru), wr, br.reshape(1, d_lru), wi,
      bi.reshape(1, d_lru), lam.reshape(1, d_lru))


_POOL_HALO = 16


def _pool_kernel(v_ref, o_ref, tail_ref):
    t_len, c = v_ref.shape
    gd = c // len(POOL_WINDOWS)
    t = pl.program_id(1)

    @pl.when(t == 0)
    def _():
        tail_ref[...] = jnp.zeros_like(tail_ref)

    v = v_ref[...]
    ext = jnp.concatenate([tail_ref[...], v], axis=0)
    tail_ref[...] = v[t_len - _POOL_HALO:]
    pos1 = t * t_len + lax.broadcasted_iota(jnp.int32, (t_len, 1), 0) + 1
    for g, w in enumerate(POOL_WINDOWS):
        s = ext[:, g * gd:(g + 1) * gd]
        span = 1
        while span < w:
            s = s + pltpu.roll(s, span, 0)
            span *= 2
        count = jnp.minimum(pos1, w).astype(f32)
        o_ref[:, g * gd:(g + 1) * gd] = (
            s[_POOL_HALO:] / count - v[:, g * gd:(g + 1) * gd]).astype(o_ref.dtype)


def _pool_branch(v, bsz, seq):
    n, c = v.shape
    assert max(POOL_WINDOWS) <= _POOL_HALO
    t_len = min(POOL_T, seq)
    nt = seq // t_len
    return pl.pallas_call(
        _pool_kernel,
        grid=(bsz, nt),
        in_specs=[pl.BlockSpec((t_len, c), lambda b, t: (b * nt + t, 0))],
        out_specs=pl.BlockSpec((t_len, c), lambda b, t: (b * nt + t, 0)),
        out_shape=jax.ShapeDtypeStruct((n, c), bf16),
        scratch_shapes=[pltpu.VMEM((_POOL_HALO, c), f32)],
        compiler_params=_cparams("arbitrary", "arbitrary"),
        name="pool",
    )(v)


_HI16 = 0xFFFF0000


def _pack_bf16_pairs(xb):
    half = xb.shape[1] // 2
    bits = lax.bitcast_convert_type(xb.astype(f32), jnp.uint32)
    return (bits[:, :half] >> 16) | (bits[:, half:] & jnp.uint32(_HI16))


def _unpack_bf16_pairs(p):
    lo = lax.bitcast_convert_type(p << 16, f32)
    hi = lax.bitcast_convert_type(p & jnp.uint32(_HI16), f32)
    return jnp.concatenate([lo, hi], axis=1)


def _router_kernel(x_ref, g_ref, wc_ref, bc_ref, h_ref, ids_ref, wcol_ref, counts_ref, base_ref, *,
                   n_groups, per_group):
    x = x_ref[...]
    ms = jnp.mean(x * x, axis=-1, keepdims=True)
    h = x * lax.rsqrt(ms + NORM_EPS) * g_ref[...]

    hh = h.astype(bf16)
    h_ref[...] = _pack_bf16_pairs(hh)
    hl = (h - hh.astype(f32)).astype(bf16)
    wc = wc_ref[...]
    wh = wc.astype(bf16)
    wl = (wc - wh.astype(f32)).astype(bf16)
    logits = (jnp.dot(hh, wh, preferred_element_type=f32)
              + jnp.dot(hl, wh, preferred_element_type=f32)
              + jnp.dot(hh, wl, preferred_element_type=f32))
    lt = logits.T + bc_ref[...]

    tm = x.shape[0]
    row = lax.broadcasted_iota(jnp.int32, (per_group, tm), 0)

    def softmax_rows(z):
        e = jnp.exp(z - jnp.max(z, axis=0, keepdims=True))
        return e / jnp.sum(e, axis=0, keepdims=True)

    def top1(p):
        pmax = jnp.max(p, axis=0, keepdims=True)
        idx = jnp.min(jnp.where(p == pmax, row, per_group), axis=0, keepdims=True)
        return pmax, idx

    assert n_groups == per_group == SUBLANES
    g_p, g_idx = top1(softmax_rows(lt[0:n_groups]))
    sel = jnp.zeros((per_group, tm), f32)
    for g in range(n_groups):
        lo = n_groups + g * per_group
        sel = jnp.where(g_idx == g, lt[lo:lo + per_group], sel)
    p = softmax_rows(sel)
    p1, i1 = top1(p)
    p2, i2 = top1(jnp.where(row == i1, -1.0, p))
    den = p1 + p2
    w1 = g_p * p1 / den
    w2 = g_p * p2 / den
    id1 = g_idx * per_group + i1
    id2 = g_idx * per_group + i2
    lane_row = lax.broadcasted_iota(jnp.int32, (LANES, tm), 0)
    wcol_ref[...] = jnp.where(lane_row == 0, w1, jnp.where(lane_row == 1, w2, 0.0)).T

    @pl.when(pl.program_id(0) == 0)
    def _():
        base_ref[...] = jnp.zeros_like(base_ref)

    n_exp = n_groups * per_group
    e_iota = lax.broadcasted_iota(jnp.int32, (n_exp, tm), 0)
    hot1 = e_iota == id1
    hot2 = e_iota == id2
    hot = jnp.concatenate([hot1, hot2], axis=0).astype(f32).astype(bf16)
    src = lax.broadcasted_iota(jnp.int32, (tm, 2 * tm), 0)
    dst = lax.broadcasted_iota(jnp.int32, (tm, 2 * tm), 1)
    tri_ones = ((src < dst) | (dst >= tm)).astype(f32).astype(bf16)
    cnt = jnp.dot(hot, tri_ones, preferred_element_type=f32)
    base = base_ref[...]
    before1 = cnt[:n_exp, :tm] + base
    tot1 = cnt[:n_exp, tm:]
    before2 = cnt[n_exp:, :tm] + base + tot1
    rank1 = jnp.sum(jnp.where(hot1, before1, 0.0), axis=0, keepdims=True).astype(jnp.int32)
    rank2 = jnp.sum(jnp.where(hot2, before2, 0.0), axis=0, keepdims=True).astype(jnp.int32)
    base = base + tot1 + cnt[n_exp:, tm:]
    base_ref[...] = base
    counts_ref[...] = base
    ids_ref[...] = jnp.where(row == 0, id1, jnp.where(row == 1, id2,
                             jnp.where(row == 2, rank1, jnp.where(row == 3, rank2, 0))))


def _router(x1, g, wg, bg, we, be):
    n, d = x1.shape
    n_groups = wg.shape[1]
    per_group = we.shape[1] // n_groups
    n_exp = we.shape[1]
    ncat = n_groups + n_exp
    assert ncat <= LANES and n_exp % SUBLANES == 0
    wc = jnp.pad(jnp.concatenate([wg, we], axis=1), ((0, 0), (0, LANES - ncat)))
    bc = jnp.pad(jnp.concatenate([bg, be]), (0, LANES - ncat)).reshape(LANES, 1)
    tm = min(ROUTER_TM, n)
    return pl.pallas_call(
        functools.partial(_router_kernel, n_groups=n_groups, per_group=per_group),
        grid=(n // tm,),
        in_specs=[pl.BlockSpec((tm, d), lambda i: (i, 0)),
                  pl.BlockSpec((1, d), lambda i: (0, 0)),
                  pl.BlockSpec((d, LANES), lambda i: (0, 0)),
                  pl.BlockSpec((LANES, 1), lambda i: (0, 0))],
        out_specs=[pl.BlockSpec((tm, d // 2), lambda i: (i, 0)),
                   pl.BlockSpec((SUBLANES, tm), lambda i: (0, i)),
                   pl.BlockSpec((tm, LANES), lambda i: (i, 0)),
                   pl.BlockSpec((n_exp, tm), lambda i: (0, 0))],
        out_shape=[jax.ShapeDtypeStruct((n, d // 2), jnp.uint32),
                   jax.ShapeDtypeStruct((SUBLANES, n), jnp.int32),
                   jax.ShapeDtypeStruct((n, LANES), f32),
                   jax.ShapeDtypeStruct((n_exp, tm), f32)],
        scratch_shapes=[pltpu.VMEM((n_exp, tm), f32)],
        compiler_params=_cparams("arbitrary"),
        name="norm_router",
    )(x1, g.reshape(1, d), wc, bc)


def _routing_tables(counts, blk, n_blocks):
    n_experts = counts.shape[0]
    padded = (counts + blk - 1) // blk * blk
    e = jnp.arange(n_experts, dtype=jnp.int32)
    padded_ends = jnp.sum(jnp.where(e[None, :] <= e[:, None], padded[None, :], 0), axis=1)
    padded_offsets = padded_ends - padded
    block_start = jnp.arange(n_blocks, dtype=jnp.int32) * blk
    block_expert = jnp.minimum(
        jnp.sum((padded_ends[None, :] <= block_start[:, None]).astype(jnp.int32), axis=1),
        n_experts - 1)
    n_used = (padded_ends[n_experts - 1:] // blk).astype(jnp.int32)
    return padded_offsets, padded_offsets + counts, padded - counts, block_expert, n_used


def _tile_index_table(ids, tm):
    n = ids.shape[1]
    rows = 2 * TOP_K
    return ids[:rows].reshape(rows, n // tm, tm).transpose(1, 0, 2).reshape(n // tm, 1, rows * tm)


_DMA_UNROLL = 8


def _dispatch_kernel(off_ref, pad_start_ref, pad_count_ref, idr_ref, h_hbm, xs_hbm, zero_ref, sem,
                     zsem):
    i = pl.program_id(0)
    tm = idr_ref.shape[2] // (2 * TOP_K)
    slot = i % 2

    def issue(g, carry):
        for u in range(_DMA_UNROLL):
            r = g * _DMA_UNROLL + u
            for k in range(TOP_K):
                dst = off_ref[idr_ref[0, 0, k * tm + r]] + idr_ref[0, 0, (TOP_K + k) * tm + r]
                pltpu.make_async_copy(h_hbm.at[pl.ds(i * tm + r, 1)], xs_hbm.at[pl.ds(dst, 1)],
                                      sem.at[slot]).start()
        return carry
    lax.fori_loop(0, tm // _DMA_UNROLL, issue, 0)

    def wait_tile(s):
        pltpu.make_async_copy(h_hbm.at[pl.ds(0, TOP_K * tm)], xs_hbm.at[pl.ds(0, TOP_K * tm)],
                              sem.at[s]).wait()

    @pl.when(i == 0)
    def _():
        zero_ref[...] = jnp.zeros_like(zero_ref)

        def pad_copy(e, j):
            return pltpu.make_async_copy(zero_ref.at[pl.ds(0, 1)],
                                         xs_hbm.at[pl.ds(pad_start_ref[e] + j, 1)], zsem.at[0])

        def start_expert(e, carry):
            def start_row(j, c):
                pad_copy(e, j).start()
                return c
            return lax.fori_loop(0, pad_count_ref[e], start_row, carry)

        def wait_expert(e, carry):
            def wait_row(j, c):
                pad_copy(e, j).wait()
                return c
            return lax.fori_loop(0, pad_count_ref[e], wait_row, carry)

        n_experts = pad_start_ref.shape[0]
        lax.fori_loop(0, n_experts, start_expert, 0)
        lax.fori_loop(0, n_experts, wait_expert, 0)

    @pl.when(i > 0)
    def _():
        wait_tile(1 - slot)

    @pl.when(i == pl.num_programs(0) - 1)
    def _():
        wait_tile(slot)


def _dispatch(h_packed, ids, padded_offsets, pad_start, pad_count, max_rows):
    n, half = h_packed.shape
    tm = min(DISPATCH_TM, n)
    idr = _tile_index_table(ids, tm)
    grid_spec = pltpu.PrefetchScalarGridSpec(
        num_scalar_prefetch=3,
        grid=(n // tm,),
        in_specs=[pl.BlockSpec((1, 1, 2 * TOP_K * tm), lambda i, *_: (i, 0, 0),
                               memory_space=pltpu.SMEM),
                  pl.BlockSpec(memory_space=pl.ANY)],
        out_specs=pl.BlockSpec(memory_space=pl.ANY),
        scratch_shapes=[pltpu.VMEM((SUBLANES, half), jnp.uint32), pltpu.SemaphoreType.DMA((2,)),
                        pltpu.SemaphoreType.DMA((1,))],
    )
    return pl.pallas_call(
        _dispatch_kernel,
        grid_spec=grid_spec,
        out_shape=jax.ShapeDtypeStruct((max_rows, half), jnp.uint32),
        compiler_params=_cparams("arbitrary"),
        name="dispatch",
    )(padded_offsets, pad_start, pad_count, idr, h_packed)


def _moe_kernel(bexp_ref, nused_ref, x_ref, wgu_ref, wdn_ref, o_ref):
    del bexp_ref
    b = pl.program_id(0)
    n_used = nused_ref[0]

    @pl.when(b < n_used)
    def _():
        x = _unpack_bf16_pairs(x_ref[...]).astype(bf16)
        gu = jnp.dot(x, wgu_ref[0].astype(bf16), preferred_element_type=f32)
        f = gu.shape[1] // 2
        act = (jax.nn.silu(gu[:, :f]) * gu[:, f:]).astype(bf16)
        y = jnp.dot(act, wdn_ref[0].astype(bf16), preferred_element_type=f32)
        o_ref[...] = _pack_bf16_pairs(y.astype(bf16))

    @pl.when(b >= n_used)
    def _():
        o_ref[...] = jnp.zeros_like(o_ref)


def _experts(x_sorted, w_gu, w_down, block_expert, n_used, blk):
    max_rows, half = x_sorted.shape
    _, d, f2 = w_gu.shape
    assert d == 2 * half
    nb = max_rows // blk
    grid_spec = pltpu.PrefetchScalarGridSpec(
        num_scalar_prefetch=2,
        grid=(nb,),
        in_specs=[
            pl.BlockSpec((blk, half), lambda b, be, nu: (jnp.minimum(b, nu[0] - 1), 0)),
            pl.BlockSpec((1, d, f2), lambda b, be, nu: (be[b], 0, 0)),
            pl.BlockSpec((1, f2 // 2, d), lambda b, be, nu: (be[b], 0, 0),
                         pipeline_mode=pl.Buffered(1)),
        ],
        out_specs=pl.BlockSpec((blk, half), lambda b, be, nu: (jnp.minimum(b, nu[0]), 0)),
    )
    return pl.pallas_call(
        _moe_kernel,
        grid_spec=grid_spec,
        out_shape=jax.ShapeDtypeStruct((max_rows, half), jnp.uint32),
        compiler_params=pltpu.CompilerParams(dimension_semantics=("arbitrary",),
                                             vmem_limit_bytes=EXPERTS_VMEM_LIMIT_BYTES),
        name="experts",
    )(block_expert, n_used, x_sorted, w_gu, w_down)


def _start_expert_row_gather(off_ref, idr_ref, tm, y_hbm, buf, sem, slot):
    def issue(g, carry):
        for u in range(_DMA_UNROLL):
            r = g * _DMA_UNROLL + u
            for k in range(TOP_K):
                src = off_ref[idr_ref[0, 0, k * tm + r]] + idr_ref[0, 0, (TOP_K + k) * tm + r]
                pltpu.make_async_copy(y_hbm.at[pl.ds(src, 1)], buf.at[slot, pl.ds(k * tm + r, 1)],
                                      sem.at[slot]).start()
        return carry
    lax.fori_loop(0, tm // _DMA_UNROLL, issue, 0)


def _combine_kernel(off_ref, idr_cur_ref, idr_nxt_ref, x_ref, g_ref, w_ref, y_hbm, o_ref, buf, sem):
    i = pl.program_id(0)
    tm = x_ref.shape[0]
    slot = i % 2

    @pl.when(i == 0)
    def _():
        _start_expert_row_gather(off_ref, idr_cur_ref, tm, y_hbm, buf, sem, 0)

    @pl.when(i + 1 < pl.num_programs(0))
    def _():
        _start_expert_row_gather(off_ref, idr_nxt_ref, tm, y_hbm, buf, sem, 1 - slot)

    pltpu.make_async_copy(y_hbm.at[pl.ds(0, buf.shape[1])], buf.at[slot], sem.at[slot]).wait()
    w = w_ref[...]
    y = (w[:, 0:1] * _unpack_bf16_pairs(buf[slot, 0:tm])
         + w[:, 1:2] * _unpack_bf16_pairs(buf[slot, tm:2 * tm]))
    x = x_ref[...] + y
    ms = jnp.mean(x * x, axis=-1, keepdims=True)
    o_ref[...] = x * lax.rsqrt(ms + NORM_EPS) * g_ref[...]


def _combine(x1, g, y_rows, ids, wcol, padded_offsets):
    n, d = x1.shape
    half = y_rows.shape[1]
    tm = min(COMBINE_TM, n)
    nt = n // tm
    idr = _tile_index_table(ids, tm)
    grid_spec = pltpu.PrefetchScalarGridSpec(
        num_scalar_prefetch=1,
        grid=(nt,),
        in_specs=[
            pl.BlockSpec((1, 1, 2 * TOP_K * tm), lambda i, off: (i, 0, 0), memory_space=pltpu.SMEM),
            pl.BlockSpec((1, 1, 2 * TOP_K * tm), lambda i, off: (jnp.minimum(i + 1, nt - 1), 0, 0),
                         memory_space=pltpu.SMEM),
            pl.BlockSpec((tm, d), lambda i, off: (i, 0)),
            pl.BlockSpec((1, d), lambda i, off: (0, 0)),
            pl.BlockSpec((tm, LANES), lambda i, off: (i, 0)),
            pl.BlockSpec(memory_space=pl.ANY),
        ],
        out_specs=pl.BlockSpec((tm, d), lambda i, off: (i, 0)),
        scratch_shapes=[pltpu.VMEM((2, TOP_K * tm, half), jnp.uint32), pltpu.SemaphoreType.DMA((2,))],
    )
    return pl.pallas_call(
        _combine_kernel,
        grid_spec=grid_spec,
        out_shape=jax.ShapeDtypeStruct((n, d), f32),
        compiler_params=_cparams("arbitrary"),
        name="combine_norm",
    )(padded_offsets, idr, idr, x1, g.reshape(1, d), wcol, y_rows)


def _layer(x2, bsz, seq, norm_mix_g, w_in, b_gate, conv_w, conv_b, lru_wr, lru_br, lru_wi, lru_bi,
           lru_lambda, w_lru_out, pool_w, pool_scale, w_out, norm_ffn_g, router_wg, router_bg,
           router_we, router_be, exp_w_gu, exp_w_down):
    n, d = x2.shape
    d_lru = w_lru_out.shape[0]
    n_pool_groups, pool_gd, pool_od = pool_w.shape
    d_pool = n_pool_groups * pool_gd
    col_gate, col_pool, col_merge = d_lru, 2 * d_lru, 2 * d_lru + d_pool
    tn = min(MM_TN, d)
    tm = min(MM_TM, n)

    xn = _rmsnorm(x2, norm_mix_g, bf16)
    hg = _inproj_lru(xn, w_in, col_gate, conv_w, conv_b, lru_wr, lru_br, lru_wi, lru_bi, lru_lambda,
                     seq)
    v =_matmul(xn, w_in, col_pool, d_pool, f32, lambda acc: acc, name="in_proj_pool")
    mg = _matmul(xn, w_in, col_merge, 2 * d, bf16,
                 lambda acc, b: jax.nn.sigmoid(acc + b),
                 extras=(b_gate.reshape(1, 2 * d),),
                 extra_specs=(pl.BlockSpec((1, tn), lambda j, i: (0, j)),),
                 name="in_proj_merge")

    pooled = _pool_branch(v, bsz, seq)

    assert pool_od % tn == 0
    per_g = pool_od // tn

    def merge_epilogue(acc, pooled_t, pw, ps, mga, mgb):
        yb = jnp.dot(pooled_t, pw[0].astype(bf16), preferred_element_type=f32) * ps
        return mga.astype(f32) * acc + mgb.astype(f32) * yb

    merged = _matmul(
        hg, w_lru_out, 0, d, bf16, merge_epilogue,
        extras=(pooled, pool_w, pool_scale.reshape(1, d), mg, mg),
        extra_specs=(pl.BlockSpec((tm, pool_gd), lambda j, i: (i, j // per_g)),
                     pl.BlockSpec((1, pool_gd, tn), lambda j, i: (j // per_g, 0, j % per_g)),
                     pl.BlockSpec((1, tn), lambda j, i: (0, j)),
                     pl.BlockSpec((tm, tn), lambda j, i: (i, j)),
                     pl.BlockSpec((tm, tn), lambda j, i: (i, j + d // tn))),
        name="lru_out_merge")

    x1 = _matmul(merged, w_out, 0, d, f32, lambda acc, xr: xr + acc,
                 extras=(x2,), extra_specs=(pl.BlockSpec((tm, tn), lambda j, i: (i, j)),),
                 name="out_proj")

    h2, ids, wcol, counts_f = _router(x1, norm_ffn_g, router_wg, router_bg, router_we, router_be)
    n_experts = exp_w_gu.shape[0]
    max_rows = n * TOP_K + n_experts * EXPERT_BLOCK
    offsets, pad_start, pad_count, block_expert, n_used = _routing_tables(
        counts_f[:, 0].astype(jnp.int32), EXPERT_BLOCK, max_rows // EXPERT_BLOCK)
    x_sorted = _dispatch(h2, ids, offsets, pad_start, pad_count, max_rows)
    y_rows = _experts(x_sorted, exp_w_gu, exp_w_down, block_expert, n_used, EXPERT_BLOCK)
    return x1, y_rows, ids, wcol, offsets


def kernel(x, norm_mix_g, w_in, b_gate, conv_w, conv_b, lru_wr, lru_br, lru_wi, lru_bi, lru_lambda,
           w_lru_out, pool_w, pool_scale, w_out, norm_ffn_g, router_wg, router_bg, router_we,
           router_be, exp_w_gu, exp_w_down, norm_final_g):
    bsz, seq, d = x.shape
    depth = w_in.shape[0]
    assert depth == 1
    x2 = x.reshape(bsz * seq, d)
    x1, y_rows, ids, wcol, offsets = _layer(
        x2, bsz, seq, norm_mix_g[0], w_in[0], b_gate[0], conv_w[0], conv_b[0], lru_wr[0], lru_br[0],
        lru_wi[0], lru_bi[0], lru_lambda[0], w_lru_out[0], pool_w[0], pool_scale[0], w_out[0],
        norm_ffn_g[0], router_wg[0], router_bg[0], router_we[0], router_be[0], exp_w_gu[0],
        exp_w_down[0])
    out = _combine(x1, norm_final_g, y_rows, ids, wcol, offsets)
    return out.reshape(bsz, seq, d)
```

```python
import functools

import jax
import jax.numpy as jnp
from jax import lax
from jax.experimental import pallas as pl
from jax.experimental.pallas import tpu as pltpu

NORM_EPS = 1e-6
LRU_C = 8.0
POOL_WINDOWS = (2, 4, 8, 16)
TOP_K = 2

VMEM_LIMIT_BYTES = 56 * 1024 * 1024
SUBLANES = 8
LANES = 128

MM_TM = 1024
MM_TN = 512
NORM_TM = 512
LRU_TM = 1024
LRU_PARTS = 4
LRU_SUBPARTS = 4
LRU_TAIL_ROWS = 16
POOL_T = 512
ROUTER_TM = 256
EXPERT_BLOCK = 256
DISPATCH_TM = 512
EXPERTS_VMEM_LIMIT_BYTES = 60 * 1024 * 1024
COMBINE_TM = 128

f32 = jnp.float32
bf16 = jnp.bfloat16


def _cparams(*sem):
    return pltpu.CompilerParams(dimension_semantics=sem, vmem_limit_bytes=VMEM_LIMIT_BYTES)


def _rmsnorm_kernel(x_ref, g_ref, o_ref):
    x = x_ref[...]
    ms = jnp.mean(x * x, axis=-1, keepdims=True)
    o_ref[...] = (x * lax.rsqrt(ms + NORM_EPS) * g_ref[...]).astype(o_ref.dtype)


def _rmsnorm(x, g, out_dtype):
    n, d = x.shape
    return pl.pallas_call(
        _rmsnorm_kernel,
        grid=(n // NORM_TM,),
        in_specs=[pl.BlockSpec((NORM_TM, d), lambda i: (i, 0)),
                  pl.BlockSpec((1, d), lambda i: (0, 0))],
        out_specs=pl.BlockSpec((NORM_TM, d), lambda i: (i, 0)),
        out_shape=jax.ShapeDtypeStruct((n, d), out_dtype),
        compiler_params=_cparams("arbitrary"),
        name="rmsnorm",
    )(x, g.reshape(1, d))


def _mm_kernel(*refs, n_extra, epilogue):
    lhs_ref, w_ref = refs[0], refs[1]
    extras = refs[2:2 + n_extra]
    o_ref = refs[2 + n_extra]
    wbf_ref = refs[3 + n_extra]

    @pl.when(pl.program_id(1) == 0)
    def _():
        wbf_ref[...] = w_ref[...].astype(bf16)

    acc = jnp.dot(lhs_ref[...], wbf_ref[...], preferred_element_type=f32)
    o_ref[...] = epilogue(acc, *[e[...] for e in extras]).astype(o_ref.dtype)


def _matmul(lhs, w, col_off, ncols, out_dtype, epilogue, extras=(), extra_specs=(), name="mm"):
    m, k = lhs.shape
    tm, tn = min(MM_TM, m), min(MM_TN, ncols)
    assert m % tm == 0 and ncols % tn == 0 and col_off % tn == 0
    off = col_off // tn
    return pl.pallas_call(
        functools.partial(_mm_kernel, n_extra=len(extras), epilogue=epilogue),
        grid=(ncols // tn, m // tm),
        in_specs=[pl.BlockSpec((tm, k), lambda j, i: (i, 0)),
                  pl.BlockSpec((k, tn), lambda j, i: (0, j + off))] + list(extra_specs),
        out_specs=pl.BlockSpec((tm, tn), lambda j, i: (i, j)),
        out_shape=jax.ShapeDtypeStruct((m, ncols), out_dtype),
        scratch_shapes=[pltpu.VMEM((k, tn), bf16)],
        compiler_params=_cparams("arbitrary", "arbitrary"),
        name=name,
    )(lhs, w, *extras)


def _lru_conv(u, tail, cw, cb):
    kw = cw.shape[0]
    ext = jnp.concatenate([tail, u], axis=0)
    uc = ext[SUBLANES:] * cw[kw - 1:kw] + cb
    for s in range(1, kw):
        uc = uc + pltpu.roll(ext, s, 0)[SUBLANES:] * cw[kw - 1 - s:kw - s]
    return uc, u[u.shape[0] - SUBLANES:]


def _lru_recurrence(uc, r_logit, i_logit, gate, carry, neg_c_sp):
    rows, c = uc.shape
    log_a = jax.nn.sigmoid(r_logit) * neg_c_sp
    a = jnp.exp(log_a)
    b = jnp.sqrt(-jnp.tanh(log_a) * (1.0 + a * a)) * (jax.nn.sigmoid(i_logit) * uc)

    groups = rows // SUBLANES
    ae = jnp.concatenate([jnp.ones((SUBLANES, c), f32), a], axis=0)
    be = jnp.concatenate([jnp.zeros((SUBLANES, c), f32), b], axis=0)
    d = 1
    while d < SUBLANES:
        be = ae * pltpu.roll(be, d, 0) + be
        ae = ae * pltpu.roll(ae, d, 0)
        d *= 2
    a3 = ae[SUBLANES:].reshape(groups, SUBLANES, c)
    b3 = be[SUBLANES:].reshape(groups, SUBLANES, c)
    m = 1
    while m < groups:
        b3 = jnp.concatenate([b3[:m], a3[m:] * b3[:-m] + b3[m:]], axis=0)
        a3 = jnp.concatenate([a3[:m], a3[m:] * a3[:-m]], axis=0)
        m *= 2
    h = b3.reshape(rows, c) + a3.reshape(rows, c) * carry
    return h * jax.nn.gelu(gate), h[rows - 1:rows]


def _inproj_lru_kernel(xn_ref, wu_ref, wg_ref, cw_ref, cb_ref, wr_ref, br_ref, wi_ref, bi_ref,
                       lam_ref, o_ref, wbf_ref, tail_ref, carry_ref, *, tiles_per_seq, parts):
    hd = wu_ref.shape[1]
    i = pl.program_id(1)

    @pl.when(i == 0)
    def _():
        wbf_ref[:, :hd] = wu_ref[...].astype(bf16)
        wbf_ref[:, hd:] = wg_ref[...].astype(bf16)

    @pl.when(i % tiles_per_seq == 0)
    def _():
        tail_ref[...] = jnp.zeros_like(tail_ref)
        carry_ref[...] = jnp.zeros_like(carry_ref)

    cw, cb = cw_ref[...], cb_ref[...]
    wr, wi = wr_ref[0].astype(bf16), wi_ref[0].astype(bf16)
    br, bi = br_ref[...], bi_ref[...]
    neg_c_sp = -LRU_C * jax.nn.softplus(-lam_ref[...])
    tail, carry = tail_ref[...], carry_ref[...]
    rows = xn_ref.shape[0] // parts
    sub = LRU_TAIL_ROWS
    accs = [jnp.dot(xn_ref[p * rows:(p + 1) * rows, :], wbf_ref[...], preferred_element_type=f32)
            for p in range(parts)]
    for p in range(parts):
        acc = accs[p]
        uc, tail = _lru_conv(acc[:, :hd], tail, cw, cb)
        ub = uc.astype(bf16)
        r_logit = jnp.dot(ub, wr, preferred_element_type=f32) + br
        i_logit = jnp.dot(ub, wi, preferred_element_type=f32) + bi
        for q in range(rows // sub):
            rs = slice(q * sub, (q + 1) * sub)
            hg, carry = _lru_recurrence(uc[rs], r_logit[rs], i_logit[rs], acc[rs, hd:], carry, neg_c_sp)
            o_ref[p * rows + q * sub:p * rows + (q + 1) * sub, :] = hg.astype(o_ref.dtype)
    tail_ref[...] = tail
    carry_ref[...] = carry


def _inproj_lru(xn, w_in, col_gate, conv_w, conv_b, wr, br, wi, bi, lam, seq):
    n, k = xn.shape
    heads, hd, _ = wr.shape
    d_lru = heads * hd
    kw = conv_w.shape[0]
    tm = min(LRU_TM, seq)
    assert seq % tm == 0 and col_gate % hd == 0 and (tm // LRU_PARTS) % SUBLANES == 0
    gate_off = col_gate // hd
    vec = lambda h, i: (0, h)
    return pl.pallas_call(
        functools.partial(_inproj_lru_kernel, tiles_per_seq=seq // tm, parts=LRU_PARTS),
        grid=(heads, n // tm),
        in_specs=[pl.BlockSpec((tm, k), lambda h, i: (i, 0)),
                  pl.BlockSpec((k, hd), lambda h, i: (0, h)),
                  pl.BlockSpec((k, hd), lambda h, i: (0, gate_off + h)),
                  pl.BlockSpec((kw, hd), vec),
                  pl.BlockSpec((1, hd), vec),
                  pl.BlockSpec((1, hd, hd), lambda h, i: (h, 0, 0)),
                  pl.BlockSpec((1, hd), vec),
                  pl.BlockSpec((1, hd, hd), lambda h, i: (h, 0, 0)),
                  pl.BlockSpec((1, hd), vec),
                  pl.BlockSpec((1, hd), vec)],
        out_specs=pl.BlockSpec((tm, hd), lambda h, i: (i, h)),
        out_shape=jax.ShapeDtypeStruct((n, d_lru), bf16),
        scratch_shapes=[pltpu.VMEM((k, 2 * hd), bf16), pltpu.VMEM((SUBLANES, hd), f32),
                        pltpu.VMEM((1, hd), f32)],
        compiler_params=_cparams("arbitrary", "arbitrary"),
        name="in_proj_rg_lru",
    )(xn, w_in, w_in, conv_w, conv_b.reshape(1, d_lru), wr, br.reshape(1, d_lru), wi,
      bi.reshape(1, d_lru), lam.reshape(1, d_lru))


_POOL_HALO = 16


def _pool_kernel(v_ref, o_ref, tail_ref):
    t_len, c = v_ref.shape
    gd = c // len(POOL_WINDOWS)
    t = pl.program_id(1)

    @pl.when(t == 0)
    def _():
        tail_ref[...] = jnp.zeros_like(tail_ref)

    v = v_ref[...]
    ext = jnp.concatenate([tail_ref[...], v], axis=0)
    tail_ref[...] = v[t_len - _POOL_HALO:]
    pos1 = t * t_len + lax.broadcasted_iota(jnp.int32, (t_len, 1), 0) + 1
    for g, w in enumerate(POOL_WINDOWS):
        s = ext[:, g * gd:(g + 1) * gd]
        span = 1
        while span < w:
            s = s + pltpu.roll(s, span, 0)
            span *= 2
        count = jnp.minimum(pos1, w).astype(f32)
        o_ref[:, g * gd:(g + 1) * gd] = (
            s[_POOL_HALO:] / count - v[:, g * gd:(g + 1) * gd]).astype(o_ref.dtype)


def _pool_branch(v, bsz, seq):
    n, c = v.shape
    assert max(POOL_WINDOWS) <= _POOL_HALO
    t_len = min(POOL_T, seq)
    nt = seq // t_len
    return pl.pallas_call(
        _pool_kernel,
        grid=(bsz, nt),
        in_specs=[pl.BlockSpec((t_len, c), lambda b, t: (b * nt + t, 0))],
        out_specs=pl.BlockSpec((t_len, c), lambda b, t: (b * nt + t, 0)),
        out_shape=jax.ShapeDtypeStruct((n, c), bf16),
        scratch_shapes=[pltpu.VMEM((_POOL_HALO, c), f32)],
        compiler_params=_cparams("arbitrary", "arbitrary"),
        name="pool",
    )(v)


_HI16 = 0xFFFF0000


def _pack_bf16_pairs(xb):
    half = xb.shape[1] // 2
    bits = lax.bitcast_convert_type(xb.astype(f32), jnp.uint32)
    return (bits[:, :half] >> 16) | (bits[:, half:] & jnp.uint32(_HI16))


def _unpack_bf16_pairs(p):
    lo = lax.bitcast_convert_type(p << 16, f32)
    hi = lax.bitcast_convert_type(p & jnp.uint32(_HI16), f32)
    return jnp.concatenate([lo, hi], axis=1)


def _router_kernel(x_ref, g_ref, wc_ref, bc_ref, h_ref, ids_ref, wcol_ref, counts_ref, base_ref, *,
                   n_groups, per_group):
    x = x_ref[...]
    ms = jnp.mean(x * x, axis=-1, keepdims=True)
    h = x * lax.rsqrt(ms + NORM_EPS) * g_ref[...]

    hh = h.astype(bf16)
    h_ref[...] = _pack_bf16_pairs(hh)
    hl = (h - hh.astype(f32)).astype(bf16)
    wc = wc_ref[...]
    wh = wc.astype(bf16)
    wl = (wc - wh.astype(f32)).astype(bf16)
    logits = (jnp.dot(hh, wh, preferred_element_type=f32)
              + jnp.dot(hl, wh, preferred_element_type=f32)
              + jnp.dot(hh, wl, preferred_element_type=f32))
    lt = logits.T + bc_ref[...]

    tm = x.shape[0]
    row = lax.broadcasted_iota(jnp.int32, (per_group, tm), 0)

    def softmax_rows(z):
        e = jnp.exp(z - jnp.max(z, axis=0, keepdims=True))
        return e / jnp.sum(e, axis=0, keepdims=True)

    def top1(p):
        pmax = jnp.max(p, axis=0, keepdims=True)
        idx = jnp.min(jnp.where(p == pmax, row, per_group), axis=0, keepdims=True)
        return pmax, idx

    assert n_groups == per_group == SUBLANES
    g_p, g_idx = top1(softmax_rows(lt[0:n_groups]))
    sel = jnp.zeros((per_group, tm), f32)
    for g in range(n_groups):
        lo = n_groups + g * per_group
        sel = jnp.where(g_idx == g, lt[lo:lo + per_group], sel)
    p = softmax_rows(sel)
    p1, i1 = top1(p)
    p2, i2 = top1(jnp.where(row == i1, -1.0, p))
    den = p1 + p2
    w1 = g_p * p1 / den
    w2 = g_p * p2 / den
    id1 = g_idx * per_group + i1
    id2 = g_idx * per_group + i2
    lane_row = lax.broadcasted_iota(jnp.int32, (LANES, tm), 0)
    wcol_ref[...] = jnp.where(lane_row == 0, w1, jnp.where(lane_row == 1, w2, 0.0)).T

    @pl.when(pl.program_id(0) == 0)
    def _():
        base_ref[...] = jnp.zeros_like(base_ref)

    n_exp = n_groups * per_group
    e_iota = lax.broadcasted_iota(jnp.int32, (n_exp, tm), 0)
    hot1 = e_iota == id1
    hot2 = e_iota == id2
    hot = jnp.concatenate([hot1, hot2], axis=0).astype(f32).astype(bf16)
    src = lax.broadcasted_iota(jnp.int32, (tm, 2 * tm), 0)
    dst = lax.broadcasted_iota(jnp.int32, (tm, 2 * tm), 1)
    tri_ones = ((src < dst) | (dst >= tm)).astype(f32).astype(bf16)
    cnt = jnp.dot(hot, tri_ones, preferred_element_type=f32)
    base = base_ref[...]
    before1 = cnt[:n_exp, :tm] + base
    tot1 = cnt[:n_exp, tm:]
    before2 = cnt[n_exp:, :tm] + base + tot1
    rank1 = jnp.sum(jnp.where(hot1, before1, 0.0), axis=0, keepdims=True).astype(jnp.int32)
    rank2 = jnp.sum(jnp.where(hot2, before2, 0.0), axis=0, keepdims=True).astype(jnp.int32)
    base = base + tot1 + cnt[n_exp:, tm:]
    base_ref[...] = base
    counts_ref[...] = base
    ids_ref[...] = jnp.where(row == 0, id1, jnp.where(row == 1, id2,
                             jnp.where(row == 2, rank1, jnp.where(row == 3, rank2, 0))))


def _router(x1, g, wg, bg, we, be):
    n, d = x1.shape
    n_groups = wg.shape[1]
    per_group = we.shape[1] // n_groups
    n_exp = we.shape[1]
    ncat = n_groups + n_exp
    assert ncat <= LANES and n_exp % SUBLANES == 0
    wc = jnp.pad(jnp.concatenate([wg, we], axis=1), ((0, 0), (0, LANES - ncat)))
    bc = jnp.pad(jnp.concatenate([bg, be]), (0, LANES - ncat)).reshape(LANES, 1)
    tm = min(ROUTER_TM, n)
    return pl.pallas_call(
        functools.partial(_router_kernel, n_groups=n_groups, per_group=per_group),
        grid=(n // tm,),
        in_specs=[pl.BlockSpec((tm, d), lambda i: (i, 0)),
                  pl.BlockSpec((1, d), lambda i: (0, 0)),
                  pl.BlockSpec((d, LANES), lambda i: (0, 0)),
                  pl.BlockSpec((LANES, 1), lambda i: (0, 0))],
        out_specs=[pl.BlockSpec((tm, d // 2), lambda i: (i, 0)),
                   pl.BlockSpec((SUBLANES, tm), lambda i: (0, i)),
                   pl.BlockSpec((tm, LANES), lambda i: (i, 0)),
                   pl.BlockSpec((n_exp, tm), lambda i: (0, 0))],
        out_shape=[jax.ShapeDtypeStruct((n, d // 2), jnp.uint32),
                   jax.ShapeDtypeStruct((SUBLANES, n), jnp.int32),
                   jax.ShapeDtypeStruct((n, LANES), f32),
                   jax.ShapeDtypeStruct((n_exp, tm), f32)],
        scratch_shapes=[pltpu.VMEM((n_exp, tm), f32)],
        compiler_params=_cparams("arbitrary"),
        name="norm_router",
    )(x1, g.reshape(1, d), wc, bc)


def _routing_tables(counts, blk, n_blocks):
    n_experts = counts.shape[0]
    padded = (counts + blk - 1) // blk * blk
    e = jnp.arange(n_experts, dtype=jnp.int32)
    padded_ends = jnp.sum(jnp.where(e[None, :] <= e[:, None], padded[None, :], 0), axis=1)
    padded_offsets = padded_ends - padded
    block_start = jnp.arange(n_blocks, dtype=jnp.int32) * blk
    block_expert = jnp.minimum(
        jnp.sum((padded_ends[None, :] <= block_start[:, None]).astype(jnp.int32), axis=1),
        n_experts - 1)
    n_used = (padded_ends[n_experts - 1:] // blk).astype(jnp.int32)
    return padded_offsets, padded_offsets + counts, padded - counts, block_expert, n_used


def _tile_index_table(ids, tm):
    n = ids.shape[1]
    rows = 2 * TOP_K
    return ids[:rows].reshape(rows, n // tm, tm).transpose(1, 0, 2).reshape(n // tm, 1, rows * tm)


_DMA_UNROLL = 8


def _dispatch_kernel(off_ref, pad_start_ref, pad_count_ref, idr_ref, h_ref, xs_hbm, zero_ref, sem,
                     zsem):
    tm = h_ref.shape[0]

    def issue(g, carry):
        for u in range(_DMA_UNROLL):
            r = g * _DMA_UNROLL + u
            for k in range(TOP_K):
                dst = off_ref[idr_ref[0, 0, k * tm + r]] + idr_ref[0, 0, (TOP_K + k) * tm + r]
                pltpu.make_async_copy(h_ref.at[pl.ds(r, 1)], xs_hbm.at[pl.ds(dst, 1)],
                                      sem.at[0]).start()
        return carry
    lax.fori_loop(0, tm // _DMA_UNROLL, issue, 0)

    @pl.when(pl.program_id(0) == 0)
    def _():
        zero_ref[...] = jnp.zeros_like(zero_ref)

        def pad_copy(e, j):
            return pltpu.make_async_copy(zero_ref.at[pl.ds(0, 1)],
                                         xs_hbm.at[pl.ds(pad_start_ref[e] + j, 1)], zsem.at[0])

        def start_expert(e, carry):
            def start_row(j, c):
                pad_copy(e, j).start()
                return c
            return lax.fori_loop(0, pad_count_ref[e], start_row, carry)

        def wait_expert(e, carry):
            def wait_row(j, c):
                pad_copy(e, j).wait()
                return c
            return lax.fori_loop(0, pad_count_ref[e], wait_row, carry)

        n_experts = pad_start_ref.shape[0]
        lax.fori_loop(0, n_experts, start_expert, 0)
        lax.fori_loop(0, n_experts, wait_expert, 0)

    for _ in range(TOP_K):
        pltpu.make_async_copy(h_ref, xs_hbm.at[pl.ds(0, tm)], sem.at[0]).wait()


def _dispatch(h_packed, ids, padded_offsets, pad_start, pad_count, max_rows):
    n, half = h_packed.shape
    tm = min(DISPATCH_TM, n)
    idr = _tile_index_table(ids, tm)
    grid_spec = pltpu.PrefetchScalarGridSpec(
        num_scalar_prefetch=3,
        grid=(n // tm,),
        in_specs=[pl.BlockSpec((1, 1, 2 * TOP_K * tm), lambda i, *_: (i, 0, 0),
                               memory_space=pltpu.SMEM),
                  pl.BlockSpec((tm, half), lambda i, *_: (i, 0))],
        out_specs=pl.BlockSpec(memory_space=pl.ANY),
        scratch_shapes=[pltpu.VMEM((SUBLANES, half), jnp.uint32), pltpu.SemaphoreType.DMA((1,)),
                        pltpu.SemaphoreType.DMA((1,))],
    )
    return pl.pallas_call(
        _dispatch_kernel,
        grid_spec=grid_spec,
        out_shape=jax.ShapeDtypeStruct((max_rows, half), jnp.uint32),
        compiler_params=_cparams("arbitrary"),
        name="dispatch",
    )(padded_offsets, pad_start, pad_count, idr, h_packed)


def _moe_kernel(bexp_ref, nused_ref, x_ref, wgu_ref, wdn_ref, o_ref):
    del bexp_ref
    b = pl.program_id(0)
    n_used = nused_ref[0]

    @pl.when(b < n_used)
    def _():
        x = _unpack_bf16_pairs(x_ref[...]).astype(bf16)
        gu = jnp.dot(x, wgu_ref[0].astype(bf16), preferred_element_type=f32)
        f = gu.shape[1] // 2
        act = (jax.nn.silu(gu[:, :f]) * gu[:, f:]).astype(bf16)
        y = jnp.dot(act, wdn_ref[0].astype(bf16), preferred_element_type=f32)
        o_ref[...] = _pack_bf16_pairs(y.astype(bf16))

    @pl.when(b >= n_used)
    def _():
        o_ref[...] = jnp.zeros_like(o_ref)


def _experts(x_sorted, w_gu, w_down, block_expert, n_used, blk):
    max_rows, half = x_sorted.shape
    _, d, f2 = w_gu.shape
    assert d == 2 * half
    nb = max_rows // blk
    grid_spec = pltpu.PrefetchScalarGridSpec(
        num_scalar_prefetch=2,
        grid=(nb,),
        in_specs=[
            pl.BlockSpec((blk, half), lambda b, be, nu: (jnp.minimum(b, nu[0] - 1), 0)),
            pl.BlockSpec((1, d, f2), lambda b, be, nu: (be[b], 0, 0)),
            pl.BlockSpec((1, f2 // 2, d), lambda b, be, nu: (be[b], 0, 0),
                         pipeline_mode=pl.Buffered(1)),
        ],
        out_specs=pl.BlockSpec((blk, half), lambda b, be, nu: (jnp.minimum(b, nu[0]), 0)),
    )
    return pl.pallas_call(
        _moe_kernel,
        grid_spec=grid_spec,
        out_shape=jax.ShapeDtypeStruct((max_rows, half), jnp.uint32),
        compiler_params=pltpu.CompilerParams(dimension_semantics=("arbitrary",),
                                             vmem_limit_bytes=EXPERTS_VMEM_LIMIT_BYTES),
        name="experts",
    )(block_expert, n_used, x_sorted, w_gu, w_down)


def _start_expert_row_gather(off_ref, idr_ref, tm, y_hbm, buf, sem, slot):
    def issue(g, carry):
        for u in range(_DMA_UNROLL):
            r = g * _DMA_UNROLL + u
            for k in range(TOP_K):
                src = off_ref[idr_ref[0, 0, k * tm + r]] + idr_ref[0, 0, (TOP_K + k) * tm + r]
                pltpu.make_async_copy(y_hbm.at[pl.ds(src, 1)], buf.at[slot, pl.ds(k * tm + r, 1)],
                                      sem.at[slot]).start()
        return carry
    lax.fori_loop(0, tm // _DMA_UNROLL, issue, 0)


def _combine_kernel(off_ref, idr_cur_ref, idr_nxt_ref, x_ref, g_ref, w_ref, y_hbm, o_ref, buf, sem):
    i = pl.program_id(0)
    tm = x_ref.shape[0]
    slot = i % 2

    @pl.when(i == 0)
    def _():
        _start_expert_row_gather(off_ref, idr_cur_ref, tm, y_hbm, buf, sem, 0)

    @pl.when(i + 1 < pl.num_programs(0))
    def _():
        _start_expert_row_gather(off_ref, idr_nxt_ref, tm, y_hbm, buf, sem, 1 - slot)

    pltpu.make_async_copy(y_hbm.at[pl.ds(0, buf.shape[1])], buf.at[slot], sem.at[slot]).wait()
    w = w_ref[...]
    y = (w[:, 0:1] * _unpack_bf16_pairs(buf[slot, 0:tm])
         + w[:, 1:2] * _unpack_bf16_pairs(buf[slot, tm:2 * tm]))
    x = x_ref[...] + y
    ms = jnp.mean(x * x, axis=-1, keepdims=True)
    o_ref[...] = x * lax.rsqrt(ms + NORM_EPS) * g_ref[...]


def _combine(x1, g, y_rows, ids, wcol, padded_offsets):
    n, d = x1.shape
    half = y_rows.shape[1]
    tm = min(COMBINE_TM, n)
    nt = n // tm
    idr = _tile_index_table(ids, tm)
    grid_spec = pltpu.PrefetchScalarGridSpec(
        num_scalar_prefetch=1,
        grid=(nt,),
        in_specs=[
            pl.BlockSpec((1, 1, 2 * TOP_K * tm), lambda i, off: (i, 0, 0), memory_space=pltpu.SMEM),
            pl.BlockSpec((1, 1, 2 * TOP_K * tm), lambda i, off: (jnp.minimum(i + 1, nt - 1), 0, 0),
                         memory_space=pltpu.SMEM),
            pl.BlockSpec((tm, d), lambda i, off: (i, 0)),
            pl.BlockSpec((1, d), lambda i, off: (0, 0)),
            pl.BlockSpec((tm, LANES), lambda i, off: (i, 0)),
            pl.BlockSpec(memory_space=pl.ANY),
        ],
        out_specs=pl.BlockSpec((tm, d), lambda i, off: (i, 0)),
        scratch_shapes=[pltpu.VMEM((2, TOP_K * tm, half), jnp.uint32), pltpu.SemaphoreType.DMA((2,))],
    )
    return pl.pallas_call(
        _combine_kernel,
        grid_spec=grid_spec,
        out_shape=jax.ShapeDtypeStruct((n, d), f32),
        compiler_params=_cparams("arbitrary"),
        name="combine_norm",
    )(padded_offsets, idr, idr, x1, g.reshape(1, d), wcol, y_rows)


def _layer(x2, bsz, seq, norm_mix_g, w_in, b_gate, conv_w, conv_b, lru_wr, lru_br, lru_wi, lru_bi,
           lru_lambda, w_lru_out, pool_w, pool_scale, w_out, norm_ffn_g, router_wg, router_bg,
           router_we, router_be, exp_w_gu, exp_w_down):
    n, d = x2.shape
    d_lru = w_lru_out.shape[0]
    n_pool_groups, pool_gd, pool_od = pool_w.shape
    d_pool = n_pool_groups * pool_gd
    col_gate, col_pool, col_merge = d_lru, 2 * d_lru, 2 * d_lru + d_pool
    tn = min(MM_TN, d)
    tm = min(MM_TM, n)

    xn = _rmsnorm(x2, norm_mix_g, bf16)
    hg = _inproj_lru(xn, w_in, col_gate, conv_w, conv_b, lru_wr, lru_br, lru_wi, lru_bi, lru_lambda,
                     seq)
    v =_matmul(xn, w_in, col_pool, d_pool, f32, lambda acc: acc, name="in_proj_pool")
    mg = _matmul(xn, w_in, col_merge, 2 * d, bf16,
                 lambda acc, b: jax.nn.sigmoid(acc + b),
                 extras=(b_gate.reshape(1, 2 * d),),
                 extra_specs=(pl.BlockSpec((1, tn), lambda j, i: (0, j)),),
                 name="in_proj_merge")

    pooled = _pool_branch(v, bsz, seq)

    assert pool_od % tn == 0
    per_g = pool_od // tn

    def merge_epilogue(acc, pooled_t, pw, ps, mga, mgb):
        yb = jnp.dot(pooled_t, pw[0].astype(bf16), preferred_element_type=f32) * ps
        return mga.astype(f32) * acc + mgb.astype(f32) * yb

    merged = _matmul(
        hg, w_lru_out, 0, d, bf16, merge_epilogue,
        extras=(pooled, pool_w, pool_scale.reshape(1, d), mg, mg),
        extra_specs=(pl.BlockSpec((tm, pool_gd), lambda j, i: (i, j // per_g)),
                     pl.BlockSpec((1, pool_gd, tn), lambda j, i: (j // per_g, 0, j % per_g)),
                     pl.BlockSpec((1, tn), lambda j, i: (0, j)),
                     pl.BlockSpec((tm, tn), lambda j, i: (i, j)),
                     pl.BlockSpec((tm, tn), lambda j, i: (i, j + d // tn))),
        name="lru_out_merge")

    x1 = _matmul(merged, w_out, 0, d, f32, lambda acc, xr: xr + acc,
                 extras=(x2,), extra_specs=(pl.BlockSpec((tm, tn), lambda j, i: (i, j)),),
                 name="out_proj")

    h2, ids, wcol, counts_f = _router(x1, norm_ffn_g, router_wg, router_bg, router_we, router_be)
    n_experts = exp_w_gu.shape[0]
    max_rows = n * TOP_K + n_experts * EXPERT_BLOCK
    offsets, pad_start, pad_count, block_expert, n_used = _routing_tables(
        counts_f[:, 0].astype(jnp.int32), EXPERT_BLOCK, max_rows // EXPERT_BLOCK)
    x_sorted = _dispatch(h2, ids, offsets, pad_start, pad_count, max_rows)
    y_rows = _experts(x_sorted, exp_w_gu, exp_w_down, block_expert, n_used, EXPERT_BLOCK)
    return x1, y_rows, ids, wcol, offsets


def kernel(x, norm_mix_g, w_in, b_gate, conv_w, conv_b, lru_wr, lru_br, lru_wi, lru_bi, lru_lambda,
           w_lru_out, pool_w, pool_scale, w_out, norm_ffn_g, router_wg, router_bg, router_we,
           router_be, exp_w_gu, exp_w_down, norm_final_g):
    bsz, seq, d = x.shape
    depth = w_in.shape[0]
    assert depth == 1
    x2 = x.reshape(bsz * seq, d)
    x1, y_rows, ids, wcol, offsets = _layer(
        x2, bsz, seq, norm_mix_g[0], w_in[0], b_gate[0], conv_w[0], conv_b[0], lru_wr[0], lru_br[0],
        lru_wi[0], lru_bi[0], lru_lambda[0], w_lru_out[0], pool_w[0], pool_scale[0], w_out[0],
        norm_ffn_g[0], router_wg[0], router_bg[0], router_we[0], router_be[0], exp_w_gu[0],
        exp_w_down[0])
    out = _combine(x1, norm_final_g, y_rows, ids, wcol, offsets)
    return out.reshape(bsz, seq, d)
```

```python
import functools

import jax
import jax.numpy as jnp
from jax import lax
from jax.experimental import pallas as pl
from jax.experimental.pallas import tpu as pltpu

NORM_EPS = 1e-6
LRU_C = 8.0
POOL_WINDOWS = (2, 4, 8, 16)
TOP_K = 2

VMEM_LIMIT_BYTES = 56 * 1024 * 1024
SUBLANES = 8
LANES = 128

MM_TM = 512
MM_TN = 1024
NORM_TM = 512
LRU_TM = 1024
LRU_PARTS = 4
LRU_SUBPARTS = 4
LRU_TAIL_ROWS = 16
POOL_T = 512
ROUTER_TM = 256
EXPERT_BLOCK = 256
DISPATCH_TM = 512
EXPERTS_VMEM_LIMIT_BYTES = 63 * 1024 * 1024
COMBINE_TM = 128

f32 = jnp.float32
bf16 = jnp.bfloat16


def _cparams(*sem):
    return pltpu.CompilerParams(dimension_semantics=sem, vmem_limit_bytes=VMEM_LIMIT_BYTES)


def _rmsnorm_kernel(x_ref, g_ref, o_ref):
    x = x_ref[...]
    ms = jnp.mean(x * x, axis=-1, keepdims=True)
    o_ref[...] = (x * lax.rsqrt(ms + NORM_EPS) * g_ref[...]).astype(o_ref.dtype)


def _rmsnorm(x, g, out_dtype):
    n, d = x.shape
    return pl.pallas_call(
        _rmsnorm_kernel,
        grid=(n // NORM_TM,),
        in_specs=[pl.BlockSpec((NORM_TM, d), lambda i: (i, 0)),
                  pl.BlockSpec((1, d), lambda i: (0, 0))],
        out_specs=pl.BlockSpec((NORM_TM, d), lambda i: (i, 0)),
        out_shape=jax.ShapeDtypeStruct((n, d), out_dtype),
        compiler_params=_cparams("arbitrary"),
        name="rmsnorm",
    )(x, g.reshape(1, d))


def _mm_kernel(*refs, n_extra, epilogue):
    lhs_ref, w_ref = refs[0], refs[1]
    extras = refs[2:2 + n_extra]
    o_ref = refs[2 + n_extra]
    wbf_ref = refs[3 + n_extra]

    @pl.when(pl.program_id(1) == 0)
    def _():
        wbf_ref[...] = w_ref[...].astype(bf16)

    acc = jnp.dot(lhs_ref[...], wbf_ref[...], preferred_element_type=f32)
    o_ref[...] = epilogue(acc, *[e[...] for e in extras]).astype(o_ref.dtype)


def _matmul(lhs, w, col_off, ncols, out_dtype, epilogue, tm, tn, extras=(), extra_specs=(),
            name="mm"):
    m, k = lhs.shape
    assert m % tm == 0 and ncols % tn == 0 and col_off % tn == 0
    off = col_off // tn
    return pl.pallas_call(
        functools.partial(_mm_kernel, n_extra=len(extras), epilogue=epilogue),
        grid=(ncols // tn, m // tm),
        in_specs=[pl.BlockSpec((tm, k), lambda j, i: (i, 0)),
                  pl.BlockSpec((k, tn), lambda j, i: (0, j + off), pipeline_mode=pl.Buffered(1))]
        + list(extra_specs),
        out_specs=pl.BlockSpec((tm, tn), lambda j, i: (i, j)),
        out_shape=jax.ShapeDtypeStruct((m, ncols), out_dtype),
        scratch_shapes=[pltpu.VMEM((k, tn), bf16)],
        compiler_params=_cparams("arbitrary", "arbitrary"),
        name=name,
    )(lhs, w, *extras)


def _lru_conv(u, tail, cw, cb):
    kw = cw.shape[0]
    ext = jnp.concatenate([tail, u], axis=0)
    uc = ext[SUBLANES:] * cw[kw - 1:kw] + cb
    for s in range(1, kw):
        uc = uc + pltpu.roll(ext, s, 0)[SUBLANES:] * cw[kw - 1 - s:kw - s]
    return uc, u[u.shape[0] - SUBLANES:]


def _lru_recurrence(uc, r_logit, i_logit, gate, carry, neg_c_sp):
    rows, c = uc.shape
    log_a = jax.nn.sigmoid(r_logit) * neg_c_sp
    a = jnp.exp(log_a)
    b = jnp.sqrt(-jnp.tanh(log_a) * (1.0 + a * a)) * (jax.nn.sigmoid(i_logit) * uc)

    groups = rows // SUBLANES
    ae = jnp.concatenate([jnp.ones((SUBLANES, c), f32), a], axis=0)
    be = jnp.concatenate([jnp.zeros((SUBLANES, c), f32), b], axis=0)
    d = 1
    while d < SUBLANES:
        be = ae * pltpu.roll(be, d, 0) + be
        ae = ae * pltpu.roll(ae, d, 0)
        d *= 2
    a3 = ae[SUBLANES:].reshape(groups, SUBLANES, c)
    b3 = be[SUBLANES:].reshape(groups, SUBLANES, c)
    m = 1
    while m < groups:
        b3 = jnp.concatenate([b3[:m], a3[m:] * b3[:-m] + b3[m:]], axis=0)
        a3 = jnp.concatenate([a3[:m], a3[m:] * a3[:-m]], axis=0)
        m *= 2
    h = b3.reshape(rows, c) + a3.reshape(rows, c) * carry
    return h * jax.nn.gelu(gate), h[rows - 1:rows]


def _inproj_lru_kernel(xn_ref, wu_ref, wg_ref, cw_ref, cb_ref, wr_ref, br_ref, wi_ref, bi_ref,
                       lam_ref, o_ref, wbf_ref, tail_ref, carry_ref, *, tiles_per_seq, parts):
    hd = wu_ref.shape[1]
    i = pl.program_id(1)

    @pl.when(i == 0)
    def _():
        wbf_ref[:, :hd] = wu_ref[...].astype(bf16)
        wbf_ref[:, hd:] = wg_ref[...].astype(bf16)

    @pl.when(i % tiles_per_seq == 0)
    def _():
        tail_ref[...] = jnp.zeros_like(tail_ref)
        carry_ref[...] = jnp.zeros_like(carry_ref)

    cw, cb = cw_ref[...], cb_ref[...]
    wr, wi = wr_ref[0].astype(bf16), wi_ref[0].astype(bf16)
    br, bi = br_ref[...], bi_ref[...]
    neg_c_sp = -LRU_C * jax.nn.softplus(-lam_ref[...])
    tail, carry = tail_ref[...], carry_ref[...]
    rows = xn_ref.shape[0] // parts
    sub = LRU_TAIL_ROWS
    accs = [jnp.dot(xn_ref[p * rows:(p + 1) * rows, :], wbf_ref[...], preferred_element_type=f32)
            for p in range(parts)]
    for p in range(parts):
        acc = accs[p]
        uc, tail = _lru_conv(acc[:, :hd], tail, cw, cb)
        ub = uc.astype(bf16)
        r_logit = jnp.dot(ub, wr, preferred_element_type=f32) + br
        i_logit = jnp.dot(ub, wi, preferred_element_type=f32) + bi
        for q in range(rows // sub):
            rs = slice(q * sub, (q + 1) * sub)
            hg, carry = _lru_recurrence(uc[rs], r_logit[rs], i_logit[rs], acc[rs, hd:], carry, neg_c_sp)
            o_ref[p * rows + q * sub:p * rows + (q + 1) * sub, :] = hg.astype(o_ref.dtype)
    tail_ref[...] = tail
    carry_ref[...] = carry


def _inproj_lru(xn, w_in, col_gate, conv_w, conv_b, wr, br, wi, bi, lam, seq):
    n, k = xn.shape
    heads, hd, _ = wr.shape
    d_lru = heads * hd
    kw = conv_w.shape[0]
    tm = min(LRU_TM, seq)
    assert seq % tm == 0 and col_gate % hd == 0 and (tm // LRU_PARTS) % SUBLANES == 0
    gate_off = col_gate // hd
    vec = lambda h, i: (0, h)
    return pl.pallas_call(
        functools.partial(_inproj_lru_kernel, tiles_per_seq=seq // tm, parts=LRU_PARTS),
        grid=(heads, n // tm),
        in_specs=[pl.BlockSpec((tm, k), lambda h, i: (i, 0)),
                  pl.BlockSpec((k, hd), lambda h, i: (0, h)),
                  pl.BlockSpec((k, hd), lambda h, i: (0, gate_off + h)),
                  pl.BlockSpec((kw, hd), vec),
                  pl.BlockSpec((1, hd), vec),
                  pl.BlockSpec((1, hd, hd), lambda h, i: (h, 0, 0)),
                  pl.BlockSpec((1, hd), vec),
                  pl.BlockSpec((1, hd, hd), lambda h, i: (h, 0, 0)),
                  pl.BlockSpec((1, hd), vec),
                  pl.BlockSpec((1, hd), vec)],
        out_specs=pl.BlockSpec((tm, hd), lambda h, i: (i, h)),
        out_shape=jax.ShapeDtypeStruct((n, d_lru), bf16),
        scratch_shapes=[pltpu.VMEM((k, 2 * hd), bf16), pltpu.VMEM((SUBLANES, hd), f32),
                        pltpu.VMEM((1, hd), f32)],
        compiler_params=_cparams("arbitrary", "arbitrary"),
        name="in_proj_rg_lru",
    )(xn, w_in, w_in, conv_w, conv_b.reshape(1, d_lru), wr, br.reshape(1, d_lru), wi,
      bi.reshape(1, d_lru), lam.reshape(1, d_lru))


_POOL_HALO = 16


def _pool_kernel(v_ref, o_ref, tail_ref):
    t_len, c = v_ref.shape
    gd = c // len(POOL_WINDOWS)
    t = pl.program_id(1)

    @pl.when(t == 0)
    def _():
        tail_ref[...] = jnp.zeros_like(tail_ref)

    v = v_ref[...]
    ext = jnp.concatenate([tail_ref[...], v], axis=0)
    tail_ref[...] = v[t_len - _POOL_HALO:]
    pos1 = t * t_len + lax.broadcasted_iota(jnp.int32, (t_len, 1), 0) + 1
    for g, w in enumerate(POOL_WINDOWS):
        s = ext[:, g * gd:(g + 1) * gd]
        span = 1
        while span < w:
            s = s + pltpu.roll(s, span, 0)
            span *= 2
        count = jnp.minimum(pos1, w).astype(f32)
        o_ref[:, g * gd:(g + 1) * gd] = (
            s[_POOL_HALO:] / count - v[:, g * gd:(g + 1) * gd]).astype(o_ref.dtype)


def _pool_branch(v, bsz, seq):
    n, c = v.shape
    assert max(POOL_WINDOWS) <= _POOL_HALO
    t_len = min(POOL_T, seq)
    nt = seq // t_len
    return pl.pallas_call(
        _pool_kernel,
        grid=(bsz, nt),
        in_specs=[pl.BlockSpec((t_len, c), lambda b, t: (b * nt + t, 0))],
        out_specs=pl.BlockSpec((t_len, c), lambda b, t: (b * nt + t, 0)),
        out_shape=jax.ShapeDtypeStruct((n, c), bf16),
        scratch_shapes=[pltpu.VMEM((_POOL_HALO, c), f32)],
        compiler_params=_cparams("arbitrary", "arbitrary"),
        name="pool",
    )(v)


_HI16 = 0xFFFF0000


def _pack_bf16_pairs(xb):
    half = xb.shape[1] // 2
    bits = lax.bitcast_convert_type(xb.astype(f32), jnp.uint32)
    return (bits[:, :half] >> 16) | (bits[:, half:] & jnp.uint32(_HI16))


def _unpack_bf16_pairs(p):
    lo = lax.bitcast_convert_type(p << 16, f32)
    hi = lax.bitcast_convert_type(p & jnp.uint32(_HI16), f32)
    return jnp.concatenate([lo, hi], axis=1)


def _router_kernel(x_ref, g_ref, wc_ref, bc_ref, h_ref, ids_ref, wcol_ref, counts_ref, base_ref, *,
                   n_groups, per_group):
    x = x_ref[...]
    ms = jnp.mean(x * x, axis=-1, keepdims=True)
    h = x * lax.rsqrt(ms + NORM_EPS) * g_ref[...]

    hh = h.astype(bf16)
    h_ref[...] = _pack_bf16_pairs(hh)
    hl = (h - hh.astype(f32)).astype(bf16)
    wc = wc_ref[...]
    wh = wc.astype(bf16)
    wl = (wc - wh.astype(f32)).astype(bf16)
    logits = (jnp.dot(hh, wh, preferred_element_type=f32)
              + jnp.dot(hl, wh, preferred_element_type=f32)
              + jnp.dot(hh, wl, preferred_element_type=f32))
    lt = logits.T + bc_ref[...]

    tm = x.shape[0]
    row = lax.broadcasted_iota(jnp.int32, (per_group, tm), 0)

    def softmax_rows(z):
        e = jnp.exp(z - jnp.max(z, axis=0, keepdims=True))
        return e / jnp.sum(e, axis=0, keepdims=True)

    def top1(p):
        pmax = jnp.max(p, axis=0, keepdims=True)
        idx = jnp.min(jnp.where(p == pmax, row, per_group), axis=0, keepdims=True)
        return pmax, idx

    assert n_groups == per_group == SUBLANES
    g_p, g_idx = top1(softmax_rows(lt[0:n_groups]))
    sel = jnp.zeros((per_group, tm), f32)
    for g in range(n_groups):
        lo = n_groups + g * per_group
        sel = jnp.where(g_idx == g, lt[lo:lo + per_group], sel)
    p = softmax_rows(sel)
    p1, i1 = top1(p)
    p2, i2 = top1(jnp.where(row == i1, -1.0, p))
    den = p1 + p2
    w1 = g_p * p1 / den
    w2 = g_p * p2 / den
    id1 = g_idx * per_group + i1
    id2 = g_idx * per_group + i2
    lane_row = lax.broadcasted_iota(jnp.int32, (LANES, tm), 0)
    wcol_ref[...] = jnp.where(lane_row == 0, w1, jnp.where(lane_row == 1, w2, 0.0)).T

    @pl.when(pl.program_id(0) == 0)
    def _():
        base_ref[...] = jnp.zeros_like(base_ref)

    n_exp = n_groups * per_group
    e_iota = lax.broadcasted_iota(jnp.int32, (n_exp, tm), 0)
    hot1 = e_iota == id1
    hot2 = e_iota == id2
    hot = jnp.concatenate([hot1, hot2], axis=0).astype(f32).astype(bf16)
    src = lax.broadcasted_iota(jnp.int32, (tm, 2 * tm), 0)
    dst = lax.broadcasted_iota(jnp.int32, (tm, 2 * tm), 1)
    tri_ones = ((src < dst) | (dst >= tm)).astype(f32).astype(bf16)
    cnt = jnp.dot(hot, tri_ones, preferred_element_type=f32)
    base = base_ref[...]
    before1 = cnt[:n_exp, :tm] + base
    tot1 = cnt[:n_exp, tm:]
    before2 = cnt[n_exp:, :tm] + base + tot1
    rank1 = jnp.sum(jnp.where(hot1, before1, 0.0), axis=0, keepdims=True).astype(jnp.int32)
    rank2 = jnp.sum(jnp.where(hot2, before2, 0.0), axis=0, keepdims=True).astype(jnp.int32)
    base = base + tot1 + cnt[n_exp:, tm:]
    base_ref[...] = base
    counts_ref[...] = base
    ids_ref[...] = jnp.where(row == 0, id1, jnp.where(row == 1, id2,
                             jnp.where(row == 2, rank1, jnp.where(row == 3, rank2, 0))))


def _router(x1, g, wg, bg, we, be):
    n, d = x1.shape
    n_groups = wg.shape[1]
    per_group = we.shape[1] // n_groups
    n_exp = we.shape[1]
    ncat = n_groups + n_exp
    assert ncat <= LANES and n_exp % SUBLANES == 0
    wc = jnp.pad(jnp.concatenate([wg, we], axis=1), ((0, 0), (0, LANES - ncat)))
    bc = jnp.pad(jnp.concatenate([bg, be]), (0, LANES - ncat)).reshape(LANES, 1)
    tm = min(ROUTER_TM, n)
    return pl.pallas_call(
        functools.partial(_router_kernel, n_groups=n_groups, per_group=per_group),
        grid=(n // tm,),
        in_specs=[pl.BlockSpec((tm, d), lambda i: (i, 0)),
                  pl.BlockSpec((1, d), lambda i: (0, 0)),
                  pl.BlockSpec((d, LANES), lambda i: (0, 0)),
                  pl.BlockSpec((LANES, 1), lambda i: (0, 0))],
        out_specs=[pl.BlockSpec((tm, d // 2), lambda i: (i, 0)),
                   pl.BlockSpec((SUBLANES, tm), lambda i: (0, i)),
                   pl.BlockSpec((tm, LANES), lambda i: (i, 0)),
                   pl.BlockSpec((n_exp, tm), lambda i: (0, 0))],
        out_shape=[jax.ShapeDtypeStruct((n, d // 2), jnp.uint32),
                   jax.ShapeDtypeStruct((SUBLANES, n), jnp.int32),
                   jax.ShapeDtypeStruct((n, LANES), f32),
                   jax.ShapeDtypeStruct((n_exp, tm), f32)],
        scratch_shapes=[pltpu.VMEM((n_exp, tm), f32)],
        compiler_params=_cparams("arbitrary"),
        name="norm_router",
    )(x1, g.reshape(1, d), wc, bc)


def _routing_tables(counts, blk, n_blocks):
    n_experts = counts.shape[0]
    padded = (counts + blk - 1) // blk * blk
    e = jnp.arange(n_experts, dtype=jnp.int32)
    padded_ends = jnp.sum(jnp.where(e[None, :] <= e[:, None], padded[None, :], 0), axis=1)
    padded_offsets = padded_ends - padded
    block_start = jnp.arange(n_blocks, dtype=jnp.int32) * blk
    block_expert = jnp.minimum(
        jnp.sum((padded_ends[None, :] <= block_start[:, None]).astype(jnp.int32), axis=1),
        n_experts - 1)
    n_used = (padded_ends[n_experts - 1:] // blk).astype(jnp.int32)
    b = jnp.arange(n_blocks, dtype=jnp.int32)
    used = b < n_used[0]
    prev_expert = jnp.concatenate([jnp.full((1,), -1, jnp.int32), block_expert[:-1]])
    first = used & (block_expert != prev_expert)
    slot = (jnp.cumsum(first.astype(jnp.int32)) - 1) % 2
    first_pos = jnp.where(first, b, n_blocks)
    next_first = lax.cummin(jnp.concatenate([first_pos[1:], jnp.full((1,), n_blocks, jnp.int32)]),
                            axis=0, reverse=True)
    next_expert = jnp.where(next_first < n_blocks,
                            block_expert[jnp.minimum(next_first, n_blocks - 1)], -1)
    schedule = jnp.stack([block_expert, slot, first.astype(jnp.int32), next_expert]).astype(jnp.int32)
    return padded_offsets, padded_offsets + counts, padded - counts, schedule, n_used


def _tile_index_table(ids, tm):
    n = ids.shape[1]
    rows = 2 * TOP_K
    return ids[:rows].reshape(rows, n // tm, tm).transpose(1, 0, 2).reshape(n // tm, 1, rows * tm)


_DMA_UNROLL = 8


def _dispatch_kernel(off_ref, pad_start_ref, pad_count_ref, idr_ref, h_ref, xs_hbm, zero_ref, sem,
                     zsem):
    tm = h_ref.shape[0]

    def issue(g, carry):
        for u in range(_DMA_UNROLL):
            r = g * _DMA_UNROLL + u
            for k in range(TOP_K):
                dst = off_ref[idr_ref[0, 0, k * tm + r]] + idr_ref[0, 0, (TOP_K + k) * tm + r]
                pltpu.make_async_copy(h_ref.at[pl.ds(r, 1)], xs_hbm.at[pl.ds(dst, 1)],
                                      sem.at[0]).start(priority=k % 2)
        return carry
    lax.fori_loop(0, tm // _DMA_UNROLL, issue, 0)

    @pl.when(pl.program_id(0) == 0)
    def _():
        zero_ref[...] = jnp.zeros_like(zero_ref)

        def pad_copy(e, j):
            return pltpu.make_async_copy(zero_ref.at[pl.ds(0, 1)],
                                         xs_hbm.at[pl.ds(pad_start_ref[e] + j, 1)], zsem.at[0])

        def start_expert(e, carry):
            def start_row(j, c):
                pad_copy(e, j).start()
                return c
            return lax.fori_loop(0, pad_count_ref[e], start_row, carry)

        def wait_expert(e, carry):
            def wait_row(j, c):
                pad_copy(e, j).wait()
                return c
            return lax.fori_loop(0, pad_count_ref[e], wait_row, carry)

        n_experts = pad_start_ref.shape[0]
        lax.fori_loop(0, n_experts, start_expert, 0)
        lax.fori_loop(0, n_experts, wait_expert, 0)

    for _ in range(TOP_K):
        pltpu.make_async_copy(h_ref, xs_hbm.at[pl.ds(0, tm)], sem.at[0]).wait()


def _dispatch(h_packed, ids, padded_offsets, pad_start, pad_count, max_rows):
    n, half = h_packed.shape
    tm = min(DISPATCH_TM, n)
    idr = _tile_index_table(ids, tm)
    grid_spec = pltpu.PrefetchScalarGridSpec(
        num_scalar_prefetch=3,
        grid=(n // tm,),
        in_specs=[pl.BlockSpec((1, 1, 2 * TOP_K * tm), lambda i, *_: (i, 0, 0),
                               memory_space=pltpu.SMEM),
                  pl.BlockSpec((tm, half), lambda i, *_: (i, 0))],
        out_specs=pl.BlockSpec(memory_space=pl.ANY),
        scratch_shapes=[pltpu.VMEM((SUBLANES, half), jnp.uint32), pltpu.SemaphoreType.DMA((1,)),
                        pltpu.SemaphoreType.DMA((1,))],
    )
    return pl.pallas_call(
        _dispatch_kernel,
        grid_spec=grid_spec,
        out_shape=jax.ShapeDtypeStruct((max_rows, half), jnp.uint32),
        compiler_params=_cparams("arbitrary"),
        name="dispatch",
    )(padded_offsets, pad_start, pad_count, idr, h_packed)


def _moe_kernel(bexp_ref, slot_ref, first_ref, next_ref, nused_ref, x_ref, wgu_hbm, wdn_hbm, o_ref,
                wgu_buf, wdn_buf, sem):
    b = pl.program_id(0)
    n_used = nused_ref[0]

    def weight_copies(e, s):
        return (pltpu.make_async_copy(wgu_hbm.at[e], wgu_buf.at[s], sem.at[s, 0]),
                pltpu.make_async_copy(wdn_hbm.at[e], wdn_buf.at[s], sem.at[s, 1]))

    @pl.when(b == 0)
    def _():
        for c in weight_copies(bexp_ref[0], 0):
            c.start()

    @pl.when(first_ref[b] == 1)
    def _():
        for c in weight_copies(bexp_ref[b], slot_ref[b]):
            c.wait()

        @pl.when(next_ref[b] >= 0)
        def _():
            for c in weight_copies(next_ref[b], 1 - slot_ref[b]):
                c.start()

    @pl.when(b < n_used)
    def _():
        s = slot_ref[b]
        x = _unpack_bf16_pairs(x_ref[...]).astype(bf16)
        gu = jnp.dot(x, wgu_buf[s].astype(bf16), preferred_element_type=f32)
        f = gu.shape[1] // 2
        act = (jax.nn.silu(gu[:, :f]) * gu[:, f:]).astype(bf16)
        y = jnp.dot(act, wdn_buf[s].astype(bf16), preferred_element_type=f32)
        o_ref[...] = _pack_bf16_pairs(y.astype(bf16))

    @pl.when(b >= n_used)
    def _():
        o_ref[...] = jnp.zeros_like(o_ref)


def _experts(x_sorted, w_gu, w_down, schedule, n_used, blk):
    max_rows, half = x_sorted.shape
    _, d, f2 = w_gu.shape
    assert d == 2 * half
    nb = max_rows // blk
    grid_spec = pltpu.PrefetchScalarGridSpec(
        num_scalar_prefetch=5,
        grid=(nb,),
        in_specs=[
            pl.BlockSpec((blk, half), lambda b, be, sl, fi, nx, nu: (jnp.minimum(b, nu[0] - 1), 0)),
            pl.BlockSpec(memory_space=pl.ANY),
            pl.BlockSpec(memory_space=pl.ANY),
        ],
        out_specs=pl.BlockSpec((blk, half), lambda b, be, sl, fi, nx, nu: (jnp.minimum(b, nu[0]), 0)),
        scratch_shapes=[pltpu.VMEM((2, d, f2), f32), pltpu.VMEM((2, f2 // 2, d), f32),
                        pltpu.SemaphoreType.DMA((2, 2))],
    )
    return pl.pallas_call(
        _moe_kernel,
        grid_spec=grid_spec,
        out_shape=jax.ShapeDtypeStruct((max_rows, half), jnp.uint32),
        compiler_params=pltpu.CompilerParams(dimension_semantics=("arbitrary",),
                                             vmem_limit_bytes=EXPERTS_VMEM_LIMIT_BYTES),
        name="experts",
    )(schedule[0], schedule[1], schedule[2], schedule[3], n_used, x_sorted, w_gu, w_down)


def _start_expert_row_gather(off_ref, idr_ref, tm, y_hbm, buf, sem, slot):
    def issue(g, carry):
        for u in range(_DMA_UNROLL):
            r = g * _DMA_UNROLL + u
            for k in range(TOP_K):
                src = off_ref[idr_ref[0, 0, k * tm + r]] + idr_ref[0, 0, (TOP_K + k) * tm + r]
                pltpu.make_async_copy(y_hbm.at[pl.ds(src, 1)], buf.at[slot, pl.ds(k * tm + r, 1)],
                                      sem.at[slot]).start(priority=k % 2)
        return carry
    lax.fori_loop(0, tm // _DMA_UNROLL, issue, 0)


def _combine_kernel(off_ref, idr_cur_ref, idr_nxt_ref, x_ref, g_ref, w_ref, y_hbm, o_ref, buf, sem):
    i = pl.program_id(0)
    tm = x_ref.shape[0]
    slot = i % 2

    @pl.when(i == 0)
    def _():
        _start_expert_row_gather(off_ref, idr_cur_ref, tm, y_hbm, buf, sem, 0)

    @pl.when(i + 1 < pl.num_programs(0))
    def _():
        _start_expert_row_gather(off_ref, idr_nxt_ref, tm, y_hbm, buf, sem, 1 - slot)

    pltpu.make_async_copy(y_hbm.at[pl.ds(0, buf.shape[1])], buf.at[slot], sem.at[slot]).wait()
    w = w_ref[...]
    y = (w[:, 0:1] * _unpack_bf16_pairs(buf[slot, 0:tm])
         + w[:, 1:2] * _unpack_bf16_pairs(buf[slot, tm:2 * tm]))
    x = x_ref[...] + y
    ms = jnp.mean(x * x, axis=-1, keepdims=True)
    o_ref[...] = x * lax.rsqrt(ms + NORM_EPS) * g_ref[...]


def _combine(x1, g, y_rows, ids, wcol, padded_offsets):
    n, d = x1.shape
    half = y_rows.shape[1]
    tm = min(COMBINE_TM, n)
    nt = n // tm
    idr = _tile_index_table(ids, tm)
    grid_spec = pltpu.PrefetchScalarGridSpec(
        num_scalar_prefetch=1,
        grid=(nt,),
        in_specs=[
            pl.BlockSpec((1, 1, 2 * TOP_K * tm), lambda i, off: (i, 0, 0), memory_space=pltpu.SMEM),
            pl.BlockSpec((1, 1, 2 * TOP_K * tm), lambda i, off: (jnp.minimum(i + 1, nt - 1), 0, 0),
                         memory_space=pltpu.SMEM),
            pl.BlockSpec((tm, d), lambda i, off: (i, 0)),
            pl.BlockSpec((1, d), lambda i, off: (0, 0)),
            pl.BlockSpec((tm, LANES), lambda i, off: (i, 0)),
            pl.BlockSpec(memory_space=pl.ANY),
        ],
        out_specs=pl.BlockSpec((tm, d), lambda i, off: (i, 0)),
        scratch_shapes=[pltpu.VMEM((2, TOP_K * tm, half), jnp.uint32), pltpu.SemaphoreType.DMA((2,))],
    )
    return pl.pallas_call(
        _combine_kernel,
        grid_spec=grid_spec,
        out_shape=jax.ShapeDtypeStruct((n, d), f32),
        compiler_params=_cparams("arbitrary"),
        name="combine_norm",
    )(padded_offsets, idr, idr, x1, g.reshape(1, d), wcol, y_rows)


def _layer(x2, bsz, seq, norm_mix_g, w_in, b_gate, conv_w, conv_b, lru_wr, lru_br, lru_wi, lru_bi,
           lru_lambda, w_lru_out, pool_w, pool_scale, w_out, norm_ffn_g, router_wg, router_bg,
           router_we, router_be, exp_w_gu, exp_w_down):
    n, d = x2.shape
    d_lru = w_lru_out.shape[0]
    n_pool_groups, pool_gd, pool_od = pool_w.shape
    d_pool = n_pool_groups * pool_gd
    col_gate, col_pool, col_merge = d_lru, 2 * d_lru, 2 * d_lru + d_pool
    tn = min(MM_TN, pool_od, d_pool)
    tm = min(MM_TM, n)

    xn = _rmsnorm(x2, norm_mix_g, bf16)
    hg = _inproj_lru(xn, w_in, col_gate, conv_w, conv_b, lru_wr, lru_br, lru_wi, lru_bi, lru_lambda,
                     seq)
    v = _matmul(xn, w_in, col_pool, d_pool, f32, lambda acc: acc, tm, tn, name="in_proj_pool")
    mg = _matmul(xn, w_in, col_merge, 2 * d, bf16,
                 lambda acc, b: jax.nn.sigmoid(acc + b), tm, tn,
                 extras=(b_gate.reshape(1, 2 * d),),
                 extra_specs=(pl.BlockSpec((1, tn), lambda j, i: (0, j)),),
                 name="in_proj_merge")

    pooled = _pool_branch(v, bsz, seq)

    assert pool_od % tn == 0
    per_g = pool_od // tn

    def merge_epilogue(acc, pooled_t, pw, ps, mga, mgb):
        yb = jnp.dot(pooled_t, pw[0].astype(bf16), preferred_element_type=f32) * ps
        return mga.astype(f32) * acc + mgb.astype(f32) * yb

    merged = _matmul(
        hg, w_lru_out, 0, d, bf16, merge_epilogue, tm, tn,
        extras=(pooled, pool_w, pool_scale.reshape(1, d), mg, mg),
        extra_specs=(pl.BlockSpec((tm, pool_gd), lambda j, i: (i, j // per_g)),
                     pl.BlockSpec((1, pool_gd, tn), lambda j, i: (j // per_g, 0, j % per_g)),
                     pl.BlockSpec((1, tn), lambda j, i: (0, j)),
                     pl.BlockSpec((tm, tn), lambda j, i: (i, j)),
                     pl.BlockSpec((tm, tn), lambda j, i: (i, j + d // tn))),
        name="lru_out_merge")

    x1 = _matmul(merged, w_out, 0, d, f32, lambda acc, xr: xr + acc, tm, tn,
                 extras=(x2,), extra_specs=(pl.BlockSpec((tm, tn), lambda j, i: (i, j)),),
                 name="out_proj")

    h2, ids, wcol, counts_f = _router(x1, norm_ffn_g, router_wg, router_bg, router_we, router_be)
    n_experts = exp_w_gu.shape[0]
    max_rows = n * TOP_K + n_experts * EXPERT_BLOCK
    offsets, pad_start, pad_count, schedule, n_used = _routing_tables(
        counts_f[:, 0].astype(jnp.int32), EXPERT_BLOCK, max_rows // EXPERT_BLOCK)
    x_sorted = _dispatch(h2, ids, offsets, pad_start, pad_count, max_rows)
    y_rows = _experts(x_sorted, exp_w_gu, exp_w_down, schedule, n_used, EXPERT_BLOCK)
    return x1, y_rows, ids, wcol, offsets


def kernel(x, norm_mix_g, w_in, b_gate, conv_w, conv_b, lru_wr, lru_br, lru_wi, lru_bi, lru_lambda,
           w_lru_out, pool_w, pool_scale, w_out, norm_ffn_g, router_wg, router_bg, router_we,
           router_be, exp_w_gu, exp_w_down, norm_final_g):
    bsz, seq, d = x.shape
    depth = w_in.shape[0]
    assert depth == 1
    x2 = x.reshape(bsz * seq, d)
    x1, y_rows, ids, wcol, offsets = _layer(
        x2, bsz, seq, norm_mix_g[0], w_in[0], b_gate[0], conv_w[0], conv_b[0], lru_wr[0], lru_br[0],
        lru_wi[0], lru_bi[0], lru_lambda[0], w_lru_out[0], pool_w[0], pool_scale[0], w_out[0],
        norm_ffn_g[0], router_wg[0], router_bg[0], router_we[0], router_be[0], exp_w_gu[0],
        exp_w_down[0])
    out = _combine(x1, norm_final_g, y_rows, ids, wcol, offsets)
    return out.reshape(bsz, seq, d)
```

```python
import functools

import jax
import jax.numpy as jnp
from jax import lax
from jax.experimental import pallas as pl
from jax.experimental.pallas import tpu as pltpu

NORM_EPS = 1e-6
LRU_C = 8.0
POOL_WINDOWS = (2, 4, 8, 16)
TOP_K = 2

VMEM_LIMIT_BYTES = 56 * 1024 * 1024
SUBLANES = 8
LANES = 128

MM_TM = 512
MM_TN = 1024
NORM_TM = 512
LRU_TM = 1024
LRU_PARTS = 4
POOL_T = 512
ROUTER_TM = 256
EXPERT_BLOCK = 256
DISPATCH_TM = 512
EXPERTS_VMEM_LIMIT_BYTES = 63 * 1024 * 1024
COMBINE_TM = 128
COMBINE_PIECES = 8

f32 = jnp.float32
bf16 = jnp.bfloat16


def _cparams(*sem):
    return pltpu.CompilerParams(dimension_semantics=sem, vmem_limit_bytes=VMEM_LIMIT_BYTES)


def _rmsnorm_kernel(x_ref, g_ref, o_ref):
    x = x_ref[...]
    ms = jnp.mean(x * x, axis=-1, keepdims=True)
    o_ref[...] = (x * lax.rsqrt(ms + NORM_EPS) * g_ref[...]).astype(o_ref.dtype)


def _rmsnorm(x, g, out_dtype):
    n, d = x.shape
    return pl.pallas_call(
        _rmsnorm_kernel,
        grid=(n // NORM_TM,),
        in_specs=[pl.BlockSpec((NORM_TM, d), lambda i: (i, 0)),
                  pl.BlockSpec((1, d), lambda i: (0, 0))],
        out_specs=pl.BlockSpec((NORM_TM, d), lambda i: (i, 0)),
        out_shape=jax.ShapeDtypeStruct((n, d), out_dtype),
        compiler_params=_cparams("arbitrary"),
        name="rmsnorm",
    )(x, g.reshape(1, d))


def _mm_kernel(*refs, n_extra, epilogue):
    lhs_ref, w_ref = refs[0], refs[1]
    extras = refs[2:2 + n_extra]
    o_ref = refs[2 + n_extra]
    wbf_ref = refs[3 + n_extra]

    @pl.when(pl.program_id(1) == 0)
    def _():
        wbf_ref[...] = w_ref[...].astype(bf16)

    acc = jnp.dot(lhs_ref[...], wbf_ref[...], preferred_element_type=f32)
    o_ref[...] = epilogue(acc, *[e[...] for e in extras]).astype(o_ref.dtype)


def _matmul(lhs, w, col_off, ncols, out_dtype, epilogue, tm, tn, extras=(), extra_specs=(),
            name="mm"):
    m, k = lhs.shape
    assert m % tm == 0 and ncols % tn == 0 and col_off % tn == 0
    off = col_off // tn
    return pl.pallas_call(
        functools.partial(_mm_kernel, n_extra=len(extras), epilogue=epilogue),
        grid=(ncols // tn, m // tm),
        in_specs=[pl.BlockSpec((tm, k), lambda j, i: (i, 0)),
                  pl.BlockSpec((k, tn), lambda j, i: (0, j + off), pipeline_mode=pl.Buffered(1))]
        + list(extra_specs),
        out_specs=pl.BlockSpec((tm, tn), lambda j, i: (i, j)),
        out_shape=jax.ShapeDtypeStruct((m, ncols), out_dtype),
        scratch_shapes=[pltpu.VMEM((k, tn), bf16)],
        compiler_params=_cparams("arbitrary", "arbitrary"),
        name=name,
    )(lhs, w, *extras)


def _lru_part(acc, tails, carry, cw, cb, wr, br, wi, bi, c_sp, u_scr, g_scr, o_scr):
    rows, c = acc.shape[0], acc.shape[1] // 2
    groups, kw, n_slabs = rows // SUBLANES, cw.shape[0], c // LANES
    grow = lax.broadcasted_iota(jnp.int32, (groups, LANES), 0)
    sub3 = lax.broadcasted_iota(jnp.int32, (groups // SUBLANES, SUBLANES, LANES), 1)
    sub8 = lax.broadcasted_iota(jnp.int32, (SUBLANES, LANES), 0)

    def piece(scr, s, j):
        return scr[s, pl.ds(j, groups, stride=SUBLANES), :]

    uc_pieces, new_tails = [], {j: [] for j in tails}
    for s in range(n_slabs):
        lanes = slice(s * LANES, (s + 1) * LANES)
        u_scr[s] = acc[:, lanes]
        g_scr[s] = acc[:, c + s * LANES:c + (s + 1) * LANES]
        u = [piece(u_scr, s, j) for j in range(SUBLANES)]
        before = {}
        for j in tails:
            before[j] = jnp.where(grow == 0, tails[j][:, lanes], pltpu.roll(u[j], 1, 0))
            new_tails[j].append(u[j][groups - 1:groups])
        ucs = []
        for j in range(SUBLANES):
            v = u[j] * cw[kw - 1:kw, lanes] + cb[:, lanes]
            for k in range(1, kw):
                src = u[j - k] if j >= k else before[j - k + SUBLANES]
                v = v + src * cw[kw - 1 - k:kw - k, lanes]
            ucs.append(v)
        uc_pieces.append(jnp.concatenate(ucs, axis=0))
    uc = jnp.concatenate(uc_pieces, axis=1)

    ub = uc.astype(bf16)
    r_logit = jnp.dot(ub, wr, preferred_element_type=f32) + br
    i_logit = jnp.dot(ub, wi, preferred_element_type=f32) + bi
    neg_log_a = jax.nn.sigmoid(r_logit) * c_sp
    a = jnp.exp(-neg_log_a)
    z = jnp.tanh(neg_log_a) * (1.0 + a * a)
    beta = jnp.where(z > 0.0, z * lax.rsqrt(z), 0.0)
    b = beta * (jax.nn.sigmoid(i_logit) * uc)

    carries = []
    for s in range(n_slabs):
        lanes = slice(s * LANES, (s + 1) * LANES)
        ps, hs = [], []
        for j in range(SUBLANES):
            aj = a[j * groups:(j + 1) * groups, lanes]
            bj = b[j * groups:(j + 1) * groups, lanes]
            hs.append(bj if j == 0 else aj * hs[-1] + bj)
            ps.append(aj if j == 0 else aj * ps[-1])
        pt = ps[-1].reshape(groups // SUBLANES, SUBLANES, LANES)
        ht = hs[-1].reshape(groups // SUBLANES, SUBLANES, LANES)
        d = 1
        while d < SUBLANES:
            keep = sub3 >= d
            ht = jnp.where(keep, pt * pltpu.roll(ht, d, 1) + ht, ht)
            pt = jnp.where(keep, pt * pltpu.roll(pt, d, 1), pt)
            d *= 2
        state = carry[:, lanes]
        incoming = []
        for q in range(groups // SUBLANES):
            after = ht[q] + pt[q] * state
            incoming.append(jnp.where(sub8 == 0, state, pltpu.roll(after, 1, 0)))
            state = after[SUBLANES - 1:SUBLANES]
        carries.append(state)
        incoming = jnp.concatenate(incoming, axis=0)
        for j in range(SUBLANES):
            h = hs[j] + ps[j] * incoming
            o_scr[s, pl.ds(j, groups, stride=SUBLANES), :] = h * jax.nn.gelu(piece(g_scr, s, j))
    out = jnp.concatenate([o_scr[s] for s in range(n_slabs)], axis=1)
    new_tails = {j: jnp.concatenate(v, axis=1) for j, v in new_tails.items()}
    return out, new_tails, jnp.concatenate(carries, axis=1)


def _inproj_lru_kernel(xn_ref, wu_ref, wg_ref, cw_ref, cb_ref, wr_ref, br_ref, wi_ref, bi_ref,
                       lam_ref, o_ref, wbf_ref, tail_ref, carry_ref, u_scr, g_scr, o_scr, *,
                       tiles_per_seq, parts):
    hd = wu_ref.shape[1]
    i = pl.program_id(1)

    @pl.when(i == 0)
    def _():
        wbf_ref[:, :hd] = wu_ref[...].astype(bf16)
        wbf_ref[:, hd:] = wg_ref[...].astype(bf16)

    @pl.when(i % tiles_per_seq == 0)
    def _():
        tail_ref[...] = jnp.zeros_like(tail_ref)
        carry_ref[...] = jnp.zeros_like(carry_ref)

    cw, cb = cw_ref[...], cb_ref[...]
    wr, wi = wr_ref[0].astype(bf16), wi_ref[0].astype(bf16)
    br, bi = br_ref[...], bi_ref[...]
    c_sp = LRU_C * jax.nn.softplus(-lam_ref[...])
    kw = cw.shape[0]
    tails = {j: tail_ref[j:j + 1, :] for j in range(SUBLANES - kw + 1, SUBLANES)}
    carry = carry_ref[...]
    rows = xn_ref.shape[0] // parts
    accs = [jnp.dot(xn_ref[p * rows:(p + 1) * rows, :], wbf_ref[...], preferred_element_type=f32)
            for p in range(parts)]
    for p in range(parts):
        hg, tails, carry = _lru_part(accs[p], tails, carry, cw, cb, wr, br, wi, bi, c_sp,
                                     u_scr, g_scr, o_scr)
        o_ref[p * rows:(p + 1) * rows, :] = hg.astype(o_ref.dtype)
    for j, row in tails.items():
        tail_ref[j:j + 1, :] = row
    carry_ref[...] = carry


def _inproj_lru(xn, w_in, col_gate, conv_w, conv_b, wr, br, wi, bi, lam, seq):
    n, k = xn.shape
    heads, hd, _ = wr.shape
    d_lru = heads * hd
    kw = conv_w.shape[0]
    tm = min(LRU_TM, seq)
    assert seq % tm == 0 and col_gate % hd == 0 and (tm // LRU_PARTS) % SUBLANES == 0
    gate_off = col_gate // hd
    vec = lambda h, i: (0, h)
    return pl.pallas_call(
        functools.partial(_inproj_lru_kernel, tiles_per_seq=seq // tm, parts=LRU_PARTS),
        grid=(heads, n // tm),
        in_specs=[pl.BlockSpec((tm, k), lambda h, i: (i, 0)),
                  pl.BlockSpec((k, hd), lambda h, i: (0, h)),
                  pl.BlockSpec((k, hd), lambda h, i: (0, gate_off + h)),
                  pl.BlockSpec((kw, hd), vec),
                  pl.BlockSpec((1, hd), vec),
                  pl.BlockSpec((1, hd, hd), lambda h, i: (h, 0, 0)),
                  pl.BlockSpec((1, hd), vec),
                  pl.BlockSpec((1, hd, hd), lambda h, i: (h, 0, 0)),
                  pl.BlockSpec((1, hd), vec),
                  pl.BlockSpec((1, hd), vec)],
        out_specs=pl.BlockSpec((tm, hd), lambda h, i: (i, h)),
        out_shape=jax.ShapeDtypeStruct((n, d_lru), bf16),
        scratch_shapes=[pltpu.VMEM((k, 2 * hd), bf16), pltpu.VMEM((SUBLANES, hd), f32),
                        pltpu.VMEM((1, hd), f32)]
        + [pltpu.VMEM((hd // LANES, tm // LRU_PARTS, LANES), f32)] * 3,
        compiler_params=_cparams("arbitrary", "arbitrary"),
        name="in_proj_rg_lru",
    )(xn, w_in, w_in, conv_w, conv_b.reshape(1, d_lru), wr, br.reshape(1, d_lru), wi,
      bi.reshape(1, d_lru), lam.reshape(1, d_lru))


_POOL_HALO = 16


def _pool_kernel(v_ref, o_ref, tail_ref):
    t_len, c = v_ref.shape
    gd = c // len(POOL_WINDOWS)
    t = pl.program_id(1)

    @pl.when(t == 0)
    def _():
        tail_ref[...] = jnp.zeros_like(tail_ref)

    v = v_ref[...]
    ext = jnp.concatenate([tail_ref[...], v], axis=0)
    tail_ref[...] = v[t_len - _POOL_HALO:]
    pos1 = t * t_len + lax.broadcasted_iota(jnp.int32, (t_len, 1), 0) + 1
    for g, w in enumerate(POOL_WINDOWS):
        s = ext[:, g * gd:(g + 1) * gd]
        span = 1
        while span < w:
            s = s + pltpu.roll(s, span, 0)
            span *= 2
        count = jnp.minimum(pos1, w).astype(f32)
        o_ref[:, g * gd:(g + 1) * gd] = (
            s[_POOL_HALO:] / count - v[:, g * gd:(g + 1) * gd]).astype(o_ref.dtype)


def _pool_branch(v, bsz, seq):
    n, c = v.shape
    assert max(POOL_WINDOWS) <= _POOL_HALO
    t_len = min(POOL_T, seq)
    nt = seq // t_len
    return pl.pallas_call(
        _pool_kernel,
        grid=(bsz, nt),
        in_specs=[pl.BlockSpec((t_len, c), lambda b, t: (b * nt + t, 0))],
        out_specs=pl.BlockSpec((t_len, c), lambda b, t: (b * nt + t, 0)),
        out_shape=jax.ShapeDtypeStruct((n, c), bf16),
        scratch_shapes=[pltpu.VMEM((_POOL_HALO, c), f32)],
        compiler_params=_cparams("arbitrary", "arbitrary"),
        name="pool",
    )(v)


_HI16 = 0xFFFF0000


def _pack_bf16_pairs(xb):
    half = xb.shape[1] // 2
    bits = lax.bitcast_convert_type(xb.astype(f32), jnp.uint32)
    return (bits[:, :half] >> 16) | (bits[:, half:] & jnp.uint32(_HI16))


def _unpack_bf16_pairs(p):
    lo = lax.bitcast_convert_type(p << 16, f32)
    hi = lax.bitcast_convert_type(p & jnp.uint32(_HI16), f32)
    return jnp.concatenate([lo, hi], axis=1)


def _router_kernel(x_ref, g_ref, wc_ref, bc_ref, h_ref, ids_ref, wcol_ref, counts_ref, base_ref, *,
                   n_groups, per_group):
    x = x_ref[...]
    ms = jnp.mean(x * x, axis=-1, keepdims=True)
    h = x * lax.rsqrt(ms + NORM_EPS) * g_ref[...]

    hh = h.astype(bf16)
    h_ref[...] = _pack_bf16_pairs(hh)
    hl = (h - hh.astype(f32)).astype(bf16)
    wc = wc_ref[...]
    wh = wc.astype(bf16)
    wl = (wc - wh.astype(f32)).astype(bf16)
    logits = (jnp.dot(hh, wh, preferred_element_type=f32)
              + jnp.dot(hl, wh, preferred_element_type=f32)
              + jnp.dot(hh, wl, preferred_element_type=f32))
    lt = logits.T + bc_ref[...]

    tm = x.shape[0]
    row = lax.broadcasted_iota(jnp.int32, (per_group, tm), 0)

    def softmax_rows(z):
        e = jnp.exp(z - jnp.max(z, axis=0, keepdims=True))
        return e / jnp.sum(e, axis=0, keepdims=True)

    def top1(p):
        pmax = jnp.max(p, axis=0, keepdims=True)
        idx = jnp.min(jnp.where(p == pmax, row, per_group), axis=0, keepdims=True)
        return pmax, idx

    assert n_groups == per_group == SUBLANES
    g_p, g_idx = top1(softmax_rows(lt[0:n_groups]))
    sel = jnp.zeros((per_group, tm), f32)
    for g in range(n_groups):
        lo = n_groups + g * per_group
        sel = jnp.where(g_idx == g, lt[lo:lo + per_group], sel)
    p = softmax_rows(sel)
    p1, i1 = top1(p)
    p2, i2 = top1(jnp.where(row == i1, -1.0, p))
    den = p1 + p2
    w1 = g_p * p1 / den
    w2 = g_p * p2 / den
    id1 = g_idx * per_group + i1
    id2 = g_idx * per_group + i2
    lane_row = lax.broadcasted_iota(jnp.int32, (LANES, tm), 0)
    wcol_ref[...] = jnp.where(lane_row == 0, w1, jnp.where(lane_row == 1, w2, 0.0)).T

    @pl.when(pl.program_id(0) == 0)
    def _():
        base_ref[...] = jnp.zeros_like(base_ref)

    n_exp = n_groups * per_group
    e_iota = lax.broadcasted_iota(jnp.int32, (n_exp, tm), 0)
    hot1 = e_iota == id1
    hot2 = e_iota == id2
    hot = jnp.concatenate([hot1, hot2], axis=0).astype(f32).astype(bf16)
    src = lax.broadcasted_iota(jnp.int32, (tm, 2 * tm), 0)
    dst = lax.broadcasted_iota(jnp.int32, (tm, 2 * tm), 1)
    tri_ones = ((src < dst) | (dst >= tm)).astype(f32).astype(bf16)
    cnt = jnp.dot(hot, tri_ones, preferred_element_type=f32)
    base = base_ref[...]
    before1 = cnt[:n_exp, :tm] + base
    tot1 = cnt[:n_exp, tm:]
    before2 = cnt[n_exp:, :tm] + base + tot1
    rank1 = jnp.sum(jnp.where(hot1, before1, 0.0), axis=0, keepdims=True).astype(jnp.int32)
    rank2 = jnp.sum(jnp.where(hot2, before2, 0.0), axis=0, keepdims=True).astype(jnp.int32)
    base = base + tot1 + cnt[n_exp:, tm:]
    base_ref[...] = base
    counts_ref[...] = base
    ids_ref[...] = jnp.where(row == 0, id1, jnp.where(row == 1, id2,
                             jnp.where(row == 2, rank1, jnp.where(row == 3, rank2, 0))))


def _router(x1, g, wg, bg, we, be):
    n, d = x1.shape
    n_groups = wg.shape[1]
    per_group = we.shape[1] // n_groups
    n_exp = we.shape[1]
    ncat = n_groups + n_exp
    assert ncat <= LANES and n_exp % SUBLANES == 0
    wc = jnp.pad(jnp.concatenate([wg, we], axis=1), ((0, 0), (0, LANES - ncat)))
    bc = jnp.pad(jnp.concatenate([bg, be]), (0, LANES - ncat)).reshape(LANES, 1)
    tm = min(ROUTER_TM, n)
    return pl.pallas_call(
        functools.partial(_router_kernel, n_groups=n_groups, per_group=per_group),
        grid=(n // tm,),
        in_specs=[pl.BlockSpec((tm, d), lambda i: (i, 0)),
                  pl.BlockSpec((1, d), lambda i: (0, 0)),
                  pl.BlockSpec((d, LANES), lambda i: (0, 0)),
                  pl.BlockSpec((LANES, 1), lambda i: (0, 0))],
        out_specs=[pl.BlockSpec((tm, d // 2), lambda i: (i, 0)),
                   pl.BlockSpec((SUBLANES, tm), lambda i: (0, i)),
                   pl.BlockSpec((tm, LANES), lambda i: (i, 0)),
                   pl.BlockSpec((n_exp, tm), lambda i: (0, 0))],
        out_shape=[jax.ShapeDtypeStruct((n, d // 2), jnp.uint32),
                   jax.ShapeDtypeStruct((SUBLANES, n), jnp.int32),
                   jax.ShapeDtypeStruct((n, LANES), f32),
                   jax.ShapeDtypeStruct((n_exp, tm), f32)],
        scratch_shapes=[pltpu.VMEM((n_exp, tm), f32)],
        compiler_params=_cparams("arbitrary"),
        name="norm_router",
    )(x1, g.reshape(1, d), wc, bc)


def _routing_tables(counts, blk, n_blocks):
    n_experts = counts.shape[0]
    padded = (counts + blk - 1) // blk * blk
    e = jnp.arange(n_experts, dtype=jnp.int32)
    padded_ends = jnp.sum(jnp.where(e[None, :] <= e[:, None], padded[None, :], 0), axis=1)
    padded_offsets = padded_ends - padded
    block_start = jnp.arange(n_blocks, dtype=jnp.int32) * blk
    block_expert = jnp.minimum(
        jnp.sum((padded_ends[None, :] <= block_start[:, None]).astype(jnp.int32), axis=1),
        n_experts - 1)
    n_used = (padded_ends[n_experts - 1:] // blk).astype(jnp.int32)
    b = jnp.arange(n_blocks, dtype=jnp.int32)
    used = b < n_used[0]
    prev_expert = jnp.concatenate([jnp.full((1,), -1, jnp.int32), block_expert[:-1]])
    first = used & (block_expert != prev_expert)
    slot = (jnp.cumsum(first.astype(jnp.int32)) - 1) % 2
    first_pos = jnp.where(first, b, n_blocks)
    next_first = lax.cummin(jnp.concatenate([first_pos[1:], jnp.full((1,), n_blocks, jnp.int32)]),
                            axis=0, reverse=True)
    next_expert = jnp.where(next_first < n_blocks,
                            block_expert[jnp.minimum(next_first, n_blocks - 1)], -1)
    schedule = jnp.stack([block_expert, slot, first.astype(jnp.int32), next_expert]).astype(jnp.int32)
    return padded_offsets, padded_offsets + counts, padded - counts, schedule, n_used


def _tile_index_table(ids, tm):
    n = ids.shape[1]
    rows = 2 * TOP_K
    return ids[:rows].reshape(rows, n // tm, tm).transpose(1, 0, 2).reshape(n // tm, 1, rows * tm)


_DMA_UNROLL = 8


def _dispatch_kernel(off_ref, pad_start_ref, pad_count_ref, idr_ref, h_ref, xs_hbm, zero_ref, sem,
                     zsem):
    tm = h_ref.shape[0]

    def issue(g, carry):
        for u in range(_DMA_UNROLL):
            r = g * _DMA_UNROLL + u
            for k in range(TOP_K):
                dst = off_ref[idr_ref[0, 0, k * tm + r]] + idr_ref[0, 0, (TOP_K + k) * tm + r]
                pltpu.make_async_copy(h_ref.at[pl.ds(r, 1)], xs_hbm.at[pl.ds(dst, 1)],
                                      sem.at[0]).start(priority=k % 2)
        return carry
    lax.fori_loop(0, tm // _DMA_UNROLL, issue, 0)

    @pl.when(pl.program_id(0) == 0)
    def _():
        zero_ref[...] = jnp.zeros_like(zero_ref)

        def pad_copy(e, j):
            return pltpu.make_async_copy(zero_ref.at[pl.ds(0, 1)],
                                         xs_hbm.at[pl.ds(pad_start_ref[e] + j, 1)], zsem.at[0])

        def start_expert(e, carry):
            def start_row(j, c):
                pad_copy(e, j).start()
                return c
            return lax.fori_loop(0, pad_count_ref[e], start_row, carry)

        def wait_expert(e, carry):
            def wait_row(j, c):
                pad_copy(e, j).wait()
                return c
            return lax.fori_loop(0, pad_count_ref[e], wait_row, carry)

        n_experts = pad_start_ref.shape[0]
        lax.fori_loop(0, n_experts, start_expert, 0)
        lax.fori_loop(0, n_experts, wait_expert, 0)

    for _ in range(TOP_K):
        pltpu.make_async_copy(h_ref, xs_hbm.at[pl.ds(0, tm)], sem.at[0]).wait()


def _dispatch(h_packed, ids, padded_offsets, pad_start, pad_count, max_rows):
    n, half = h_packed.shape
    tm = min(DISPATCH_TM, n)
    idr = _tile_index_table(ids, tm)
    grid_spec = pltpu.PrefetchScalarGridSpec(
        num_scalar_prefetch=3,
        grid=(n // tm,),
        in_specs=[pl.BlockSpec((1, 1, 2 * TOP_K * tm), lambda i, *_: (i, 0, 0),
                               memory_space=pltpu.SMEM),
                  pl.BlockSpec((tm, half), lambda i, *_: (i, 0))],
        out_specs=pl.BlockSpec(memory_space=pl.ANY),
        scratch_shapes=[pltpu.VMEM((SUBLANES, half), jnp.uint32), pltpu.SemaphoreType.DMA((1,)),
                        pltpu.SemaphoreType.DMA((1,))],
    )
    return pl.pallas_call(
        _dispatch_kernel,
        grid_spec=grid_spec,
        out_shape=jax.ShapeDtypeStruct((max_rows, half), jnp.uint32),
        compiler_params=_cparams("arbitrary"),
        name="dispatch",
    )(padded_offsets, pad_start, pad_count, idr, h_packed)


def _moe_kernel(bexp_ref, slot_ref, first_ref, next_ref, nused_ref, x_ref, wgu_hbm, wdn_hbm, o_ref,
                wgu_buf, wdn_buf, sem):
    b = pl.program_id(0)
    n_used = nused_ref[0]

    def weight_copies(e, s):
        return (pltpu.make_async_copy(wgu_hbm.at[e], wgu_buf.at[s], sem.at[s, 0]),
                pltpu.make_async_copy(wdn_hbm.at[e], wdn_buf.at[s], sem.at[s, 1]))

    @pl.when(b == 0)
    def _():
        for c in weight_copies(bexp_ref[0], 0):
            c.start()

    @pl.when(first_ref[b] == 1)
    def _():
        for c in weight_copies(bexp_ref[b], slot_ref[b]):
            c.wait()

        @pl.when(next_ref[b] >= 0)
        def _():
            for c in weight_copies(next_ref[b], 1 - slot_ref[b]):
                c.start()

    @pl.when(b < n_used)
    def _():
        s = slot_ref[b]
        x = _unpack_bf16_pairs(x_ref[...]).astype(bf16)
        gu = jnp.dot(x, wgu_buf[s].astype(bf16), preferred_element_type=f32)
        f = gu.shape[1] // 2
        act = (jax.nn.silu(gu[:, :f]) * gu[:, f:]).astype(bf16)
        y = jnp.dot(act, wdn_buf[s].astype(bf16), preferred_element_type=f32)
        o_ref[...] = _pack_bf16_pairs(y.astype(bf16))

    @pl.when(b >= n_used)
    def _():
        o_ref[...] = jnp.zeros_like(o_ref)


def _experts(x_sorted, w_gu, w_down, schedule, n_used, blk):
    max_rows, half = x_sorted.shape
    _, d, f2 = w_gu.shape
    assert d == 2 * half
    nb = max_rows // blk
    grid_spec = pltpu.PrefetchScalarGridSpec(
        num_scalar_prefetch=5,
        grid=(nb,),
        in_specs=[
            pl.BlockSpec((blk, half), lambda b, be, sl, fi, nx, nu: (jnp.minimum(b, nu[0] - 1), 0)),
            pl.BlockSpec(memory_space=pl.ANY),
            pl.BlockSpec(memory_space=pl.ANY),
        ],
        out_specs=pl.BlockSpec((blk, half), lambda b, be, sl, fi, nx, nu: (jnp.minimum(b, nu[0]), 0)),
        scratch_shapes=[pltpu.VMEM((2, d, f2), f32), pltpu.VMEM((2, f2 // 2, d), f32),
                        pltpu.SemaphoreType.DMA((2, 2))],
    )
    return pl.pallas_call(
        _moe_kernel,
        grid_spec=grid_spec,
        out_shape=jax.ShapeDtypeStruct((max_rows, half), jnp.uint32),
        compiler_params=pltpu.CompilerParams(dimension_semantics=("arbitrary",),
                                             vmem_limit_bytes=EXPERTS_VMEM_LIMIT_BYTES),
        name="experts",
    )(schedule[0], schedule[1], schedule[2], schedule[3], n_used, x_sorted, w_gu, w_down)


def _start_expert_row_copy(off_ref, idr_ref, tm, r, y_hbm, buf, sem, slot):
    for k in range(TOP_K):
        src = off_ref[idr_ref[0, 0, k * tm + r]] + idr_ref[0, 0, (TOP_K + k) * tm + r]
        pltpu.make_async_copy(y_hbm.at[pl.ds(src, 1)], buf.at[slot, pl.ds(k * tm + r, 1)],
                              sem.at[slot]).start(priority=k % 2)


def _combine_kernel(off_ref, idr_cur_ref, idr_nxt_ref, x_ref, g_ref, w_ref, y_hbm, o_ref, buf, sem):
    i = pl.program_id(0)
    last = pl.num_programs(0) - 1
    tm = x_ref.shape[0]
    slot = i % 2

    def wait_slot(s):
        pltpu.make_async_copy(y_hbm.at[pl.ds(0, buf.shape[1])], buf.at[s], sem.at[s]).wait()

    @pl.when(i == 0)
    def _():
        def issue(g, carry):
            for u in range(_DMA_UNROLL):
                _start_expert_row_copy(off_ref, idr_cur_ref, tm, g * _DMA_UNROLL + u, y_hbm, buf, sem, 0)
            return carry
        lax.fori_loop(0, tm // _DMA_UNROLL, issue, 0)

    wait_slot(slot)
    rc = tm // COMBINE_PIECES
    for c in range(COMBINE_PIECES):
        for r in range(c * rc, (c + 1) * rc):
            _start_expert_row_copy(off_ref, idr_nxt_ref, tm, r, y_hbm, buf, sem, 1 - slot)
        rows = slice(c * rc, (c + 1) * rc)
        w = w_ref[rows, :]
        y = (w[:, 0:1] * _unpack_bf16_pairs(buf[slot, c * rc:(c + 1) * rc])
             + w[:, 1:2] * _unpack_bf16_pairs(buf[slot, tm + c * rc:tm + (c + 1) * rc]))
        x = x_ref[rows, :] + y
        ms = jnp.mean(x * x, axis=-1, keepdims=True)
        o_ref[rows, :] = x * lax.rsqrt(ms + NORM_EPS) * g_ref[...]

    @pl.when(i == last)
    def _():
        wait_slot(1 - slot)


def _combine(x1, g, y_rows, ids, wcol, padded_offsets):
    n, d = x1.shape
    half = y_rows.shape[1]
    tm = min(COMBINE_TM, n)
    nt = n // tm
    idr = _tile_index_table(ids, tm)
    grid_spec = pltpu.PrefetchScalarGridSpec(
        num_scalar_prefetch=1,
        grid=(nt,),
        in_specs=[
            pl.BlockSpec((1, 1, 2 * TOP_K * tm), lambda i, off: (i, 0, 0), memory_space=pltpu.SMEM),
            pl.BlockSpec((1, 1, 2 * TOP_K * tm), lambda i, off: (jnp.minimum(i + 1, nt - 1), 0, 0),
                         memory_space=pltpu.SMEM),
            pl.BlockSpec((tm, d), lambda i, off: (i, 0)),
            pl.BlockSpec((1, d), lambda i, off: (0, 0)),
            pl.BlockSpec((tm, LANES), lambda i, off: (i, 0)),
            pl.BlockSpec(memory_space=pl.ANY),
        ],
        out_specs=pl.BlockSpec((tm, d), lambda i, off: (i, 0)),
        scratch_shapes=[pltpu.VMEM((2, TOP_K * tm, half), jnp.uint32), pltpu.SemaphoreType.DMA((2,))],
    )
    return pl.pallas_call(
        _combine_kernel,
        grid_spec=grid_spec,
        out_shape=jax.ShapeDtypeStruct((n, d), f32),
        compiler_params=_cparams("arbitrary"),
        name="combine_norm",
    )(padded_offsets, idr, idr, x1, g.reshape(1, d), wcol, y_rows)


def _layer(x2, bsz, seq, norm_mix_g, w_in, b_gate, conv_w, conv_b, lru_wr, lru_br, lru_wi, lru_bi,
           lru_lambda, w_lru_out, pool_w, pool_scale, w_out, norm_ffn_g, router_wg, router_bg,
           router_we, router_be, exp_w_gu, exp_w_down):
    n, d = x2.shape
    d_lru = w_lru_out.shape[0]
    n_pool_groups, pool_gd, pool_od = pool_w.shape
    d_pool = n_pool_groups * pool_gd
    col_gate, col_pool, col_merge = d_lru, 2 * d_lru, 2 * d_lru + d_pool
    tn = min(MM_TN, pool_od, d_pool)
    tm = min(MM_TM, n)

    xn = _rmsnorm(x2, norm_mix_g, bf16)
    hg = _inproj_lru(xn, w_in, col_gate, conv_w, conv_b, lru_wr, lru_br, lru_wi, lru_bi, lru_lambda,
                     seq)
    v = _matmul(xn, w_in, col_pool, d_pool, f32, lambda acc: acc, tm, tn, name="in_proj_pool")
    mg = _matmul(xn, w_in, col_merge, 2 * d, bf16,
                 lambda acc, b: jax.nn.sigmoid(acc + b), tm, tn,
                 extras=(b_gate.reshape(1, 2 * d),),
                 extra_specs=(pl.BlockSpec((1, tn), lambda j, i: (0, j)),),
                 name="in_proj_merge")

    pooled = _pool_branch(v, bsz, seq)

    assert pool_od % tn == 0
    per_g = pool_od // tn

    def merge_epilogue(acc, pooled_t, pw, ps, mga, mgb):
        yb = jnp.dot(pooled_t, pw[0].astype(bf16), preferred_element_type=f32) * ps
        return mga.astype(f32) * acc + mgb.astype(f32) * yb

    merged = _matmul(
        hg, w_lru_out, 0, d, bf16, merge_epilogue, tm, tn,
        extras=(pooled, pool_w, pool_scale.reshape(1, d), mg, mg),
        extra_specs=(pl.BlockSpec((tm, pool_gd), lambda j, i: (i, j // per_g)),
                     pl.BlockSpec((1, pool_gd, tn), lambda j, i: (j // per_g, 0, j % per_g)),
                     pl.BlockSpec((1, tn), lambda j, i: (0, j)),
                     pl.BlockSpec((tm, tn), lambda j, i: (i, j)),
                     pl.BlockSpec((tm, tn), lambda j, i: (i, j + d // tn))),
        name="lru_out_merge")

    x1 = _matmul(merged, w_out, 0, d, f32, lambda acc, xr: xr + acc, tm, tn,
                 extras=(x2,), extra_specs=(pl.BlockSpec((tm, tn), lambda j, i: (i, j)),),
                 name="out_proj")

    h2, ids, wcol, counts_f = _router(x1, norm_ffn_g, router_wg, router_bg, router_we, router_be)
    n_experts = exp_w_gu.shape[0]
    max_rows = n * TOP_K + n_experts * EXPERT_BLOCK
    offsets, pad_start, pad_count, schedule, n_used = _routing_tables(
        counts_f[:, 0].astype(jnp.int32), EXPERT_BLOCK, max_rows // EXPERT_BLOCK)
    x_sorted = _dispatch(h2, ids, offsets, pad_start, pad_count, max_rows)
    y_rows = _experts(x_sorted, exp_w_gu, exp_w_down, schedule, n_used, EXPERT_BLOCK)
    return x1, y_rows, ids, wcol, offsets


def kernel(x, norm_mix_g, w_in, b_gate, conv_w, conv_b, lru_wr, lru_br, lru_wi, lru_bi, lru_lambda,
           w_lru_out, pool_w, pool_scale, w_out, norm_ffn_g, router_wg, router_bg, router_we,
           router_be, exp_w_gu, exp_w_down, norm_final_g):
    bsz, seq, d = x.shape
    depth = w_in.shape[0]
    assert depth == 1
    x2 = x.reshape(bsz * seq, d)
    x1, y_rows, ids, wcol, offsets = _layer(
        x2, bsz, seq, norm_mix_g[0], w_in[0], b_gate[0], conv_w[0], conv_b[0], lru_wr[0], lru_br[0],
        lru_wi[0], lru_bi[0], lru_lambda[0], w_lru_out[0], pool_w[0], pool_scale[0], w_out[0],
        norm_ffn_g[0], router_wg[0], router_bg[0], router_we[0], router_be[0], exp_w_gu[0],
        exp_w_down[0])
    out = _combine(x1, norm_final_g, y_rows, ids, wcol, offsets)
    return out.reshape(bsz, seq, d)
```

```python
import functools

import jax
import jax.numpy as jnp
from jax import lax
from jax.experimental import pallas as pl
from jax.experimental.pallas import tpu as pltpu

NORM_EPS = 1e-6
LRU_C = 8.0
POOL_WINDOWS = (2, 4, 8, 16)
TOP_K = 2

VMEM_LIMIT_BYTES = 56 * 1024 * 1024
SUBLANES = 8
LANES = 128

MM_TM = 512
MM_TN = 1024
NORM_TM = 512
LRU_TM = 1024
LRU_PARTS = 4
POOL_T = 512
ROUTER_TM = 256
EXPERT_BLOCK = 256
DISPATCH_TM = 512
EXPERTS_VMEM_LIMIT_BYTES = 63 * 1024 * 1024
COMBINE_TM = 128
COMBINE_PIECES = 8

f32 = jnp.float32
bf16 = jnp.bfloat16


def _cparams(*sem):
    return pltpu.CompilerParams(dimension_semantics=sem, vmem_limit_bytes=VMEM_LIMIT_BYTES)


def _rmsnorm_kernel(x_ref, g_ref, o_ref):
    x = x_ref[...]
    ms = jnp.mean(x * x, axis=-1, keepdims=True)
    o_ref[...] = (x * lax.rsqrt(ms + NORM_EPS) * g_ref[...]).astype(o_ref.dtype)


def _rmsnorm(x, g, out_dtype):
    n, d = x.shape
    return pl.pallas_call(
        _rmsnorm_kernel,
        grid=(n // NORM_TM,),
        in_specs=[pl.BlockSpec((NORM_TM, d), lambda i: (i, 0)),
                  pl.BlockSpec((1, d), lambda i: (0, 0))],
        out_specs=pl.BlockSpec((NORM_TM, d), lambda i: (i, 0)),
        out_shape=jax.ShapeDtypeStruct((n, d), out_dtype),
        compiler_params=_cparams("arbitrary"),
        name="rmsnorm",
    )(x, g.reshape(1, d))


def _mm_kernel(*refs, n_extra, epilogue):
    lhs_ref, w_ref = refs[0], refs[1]
    extras = refs[2:2 + n_extra]
    o_ref = refs[2 + n_extra]
    wbf_ref = refs[3 + n_extra]

    @pl.when(pl.program_id(1) == 0)
    def _():
        wbf_ref[...] = w_ref[...].astype(bf16)

    acc = jnp.dot(lhs_ref[...], wbf_ref[...], preferred_element_type=f32)
    o_ref[...] = epilogue(acc, *[e[...] for e in extras]).astype(o_ref.dtype)


def _matmul(lhs, w, col_off, ncols, out_dtype, epilogue, tm, tn, extras=(), extra_specs=(),
            name="mm"):
    m, k = lhs.shape
    assert m % tm == 0 and ncols % tn == 0 and col_off % tn == 0
    off = col_off // tn
    return pl.pallas_call(
        functools.partial(_mm_kernel, n_extra=len(extras), epilogue=epilogue),
        grid=(ncols // tn, m // tm),
        in_specs=[pl.BlockSpec((tm, k), lambda j, i: (i, 0)),
                  pl.BlockSpec((k, tn), lambda j, i: (0, j + off), pipeline_mode=pl.Buffered(1))]
        + list(extra_specs),
        out_specs=pl.BlockSpec((tm, tn), lambda j, i: (i, j)),
        out_shape=jax.ShapeDtypeStruct((m, ncols), out_dtype),
        scratch_shapes=[pltpu.VMEM((k, tn), bf16)],
        compiler_params=_cparams("arbitrary", "arbitrary"),
        name=name,
    )(lhs, w, *extras)


def _lru_part(acc, tails, carry, cw, cb, wr, br, wi, bi, c_sp, u_scr, g_scr, o_scr):
    rows, c = acc.shape[0], acc.shape[1] // 2
    groups, kw, n_slabs = rows // SUBLANES, cw.shape[0], c // LANES
    grow = lax.broadcasted_iota(jnp.int32, (groups, LANES), 0)
    sub3 = lax.broadcasted_iota(jnp.int32, (groups // SUBLANES, SUBLANES, LANES), 1)
    sub8 = lax.broadcasted_iota(jnp.int32, (SUBLANES, LANES), 0)

    def piece(scr, s, j):
        return scr[s, pl.ds(j, groups, stride=SUBLANES), :]

    uc_pieces, new_tails = [], {j: [] for j in tails}
    for s in range(n_slabs):
        lanes = slice(s * LANES, (s + 1) * LANES)
        u_scr[s] = acc[:, lanes]
        g_scr[s] = acc[:, c + s * LANES:c + (s + 1) * LANES]
        u = [piece(u_scr, s, j) for j in range(SUBLANES)]
        before = {}
        for j in tails:
            before[j] = jnp.where(grow == 0, tails[j][:, lanes], pltpu.roll(u[j], 1, 0))
            new_tails[j].append(u[j][groups - 1:groups])
        ucs = []
        for j in range(SUBLANES):
            v = u[j] * cw[kw - 1:kw, lanes] + cb[:, lanes]
            for k in range(1, kw):
                src = u[j - k] if j >= k else before[j - k + SUBLANES]
                v = v + src * cw[kw - 1 - k:kw - k, lanes]
            ucs.append(v)
        uc_pieces.append(jnp.concatenate(ucs, axis=0))
    uc = jnp.concatenate(uc_pieces, axis=1)

    ub = uc.astype(bf16)
    r_logit = jnp.dot(ub, wr, preferred_element_type=f32) + br
    i_logit = jnp.dot(ub, wi, preferred_element_type=f32) + bi
    neg_log_a = jax.nn.sigmoid(r_logit) * c_sp
    a = jnp.exp(-neg_log_a)
    z = jnp.tanh(neg_log_a) * (1.0 + a * a)
    beta = jnp.where(z > 0.0, z * lax.rsqrt(z), 0.0)
    b = beta * (jax.nn.sigmoid(i_logit) * uc)

    carries = []
    for s in range(n_slabs):
        lanes = slice(s * LANES, (s + 1) * LANES)
        ps, hs = [], []
        for j in range(SUBLANES):
            aj = a[j * groups:(j + 1) * groups, lanes]
            bj = b[j * groups:(j + 1) * groups, lanes]
            hs.append(bj if j == 0 else aj * hs[-1] + bj)
            ps.append(aj if j == 0 else aj * ps[-1])
        pt = ps[-1].reshape(groups // SUBLANES, SUBLANES, LANES)
        ht = hs[-1].reshape(groups // SUBLANES, SUBLANES, LANES)
        d = 1
        while d < SUBLANES:
            keep = sub3 >= d
            ht = jnp.where(keep, pt * pltpu.roll(ht, d, 1) + ht, ht)
            pt = jnp.where(keep, pt * pltpu.roll(pt, d, 1), pt)
            d *= 2
        state = carry[:, lanes]
        incoming = []
        for q in range(groups // SUBLANES):
            after = ht[q] + pt[q] * state
            incoming.append(jnp.where(sub8 == 0, state, pltpu.roll(after, 1, 0)))
            state = after[SUBLANES - 1:SUBLANES]
        carries.append(state)
        incoming = jnp.concatenate(incoming, axis=0)
        for j in range(SUBLANES):
            h = hs[j] + ps[j] * incoming
            o_scr[s, pl.ds(j, groups, stride=SUBLANES), :] = h * jax.nn.gelu(piece(g_scr, s, j))
    out = jnp.concatenate([o_scr[s] for s in range(n_slabs)], axis=1)
    new_tails = {j: jnp.concatenate(v, axis=1) for j, v in new_tails.items()}
    return out, new_tails, jnp.concatenate(carries, axis=1)


def _inproj_lru_kernel(xn_ref, wu_ref, wg_ref, cw_ref, cb_ref, wr_ref, br_ref, wi_ref, bi_ref,
                       lam_ref, o_ref, wbf_ref, tail_ref, carry_ref, u_scr, g_scr, o_scr, *,
                       tiles_per_seq, parts):
    hd = wu_ref.shape[1]
    i = pl.program_id(1)

    @pl.when(i == 0)
    def _():
        wbf_ref[:, :hd] = wu_ref[...].astype(bf16)
        wbf_ref[:, hd:] = wg_ref[...].astype(bf16)

    @pl.when(i % tiles_per_seq == 0)
    def _():
        tail_ref[...] = jnp.zeros_like(tail_ref)
        carry_ref[...] = jnp.zeros_like(carry_ref)

    cw, cb = cw_ref[...], cb_ref[...]
    wr, wi = wr_ref[0].astype(bf16), wi_ref[0].astype(bf16)
    br, bi = br_ref[...], bi_ref[...]
    c_sp = LRU_C * jax.nn.softplus(-lam_ref[...])
    kw = cw.shape[0]
    tails = {j: tail_ref[j:j + 1, :] for j in range(SUBLANES - kw + 1, SUBLANES)}
    carry = carry_ref[...]
    rows = xn_ref.shape[0] // parts
    accs = [jnp.dot(xn_ref[p * rows:(p + 1) * rows, :], wbf_ref[...], preferred_element_type=f32)
            for p in range(parts)]
    for p in range(parts):
        hg, tails, carry = _lru_part(accs[p], tails, carry, cw, cb, wr, br, wi, bi, c_sp,
                                     u_scr, g_scr, o_scr)
        o_ref[p * rows:(p + 1) * rows, :] = hg.astype(o_ref.dtype)
    for j, row in tails.items():
        tail_ref[j:j + 1, :] = row
    carry_ref[...] = carry


def _inproj_lru(xn, w_in, col_gate, conv_w, conv_b, wr, br, wi, bi, lam, seq):
    n, k = xn.shape
    heads, hd, _ = wr.shape
    d_lru = heads * hd
    kw = conv_w.shape[0]
    tm = min(LRU_TM, seq)
    assert seq % tm == 0 and col_gate % hd == 0 and (tm // LRU_PARTS) % SUBLANES == 0
    gate_off = col_gate // hd
    vec = lambda h, i: (0, h)
    return pl.pallas_call(
        functools.partial(_inproj_lru_kernel, tiles_per_seq=seq // tm, parts=LRU_PARTS),
        grid=(heads, n // tm),
        in_specs=[pl.BlockSpec((tm, k), lambda h, i: (i, 0)),
                  pl.BlockSpec((k, hd), lambda h, i: (0, h)),
                  pl.BlockSpec((k, hd), lambda h, i: (0, gate_off + h)),
                  pl.BlockSpec((kw, hd), vec),
                  pl.BlockSpec((1, hd), vec),
                  pl.BlockSpec((1, hd, hd), lambda h, i: (h, 0, 0)),
                  pl.BlockSpec((1, hd), vec),
                  pl.BlockSpec((1, hd, hd), lambda h, i: (h, 0, 0)),
                  pl.BlockSpec((1, hd), vec),
                  pl.BlockSpec((1, hd), vec)],
        out_specs=pl.BlockSpec((tm, hd), lambda h, i: (i, h)),
        out_shape=jax.ShapeDtypeStruct((n, d_lru), bf16),
        scratch_shapes=[pltpu.VMEM((k, 2 * hd), bf16), pltpu.VMEM((SUBLANES, hd), f32),
                        pltpu.VMEM((1, hd), f32)]
        + [pltpu.VMEM((hd // LANES, tm // LRU_PARTS, LANES), f32)] * 3,
        compiler_params=_cparams("arbitrary", "arbitrary"),
        name="in_proj_rg_lru",
    )(xn, w_in, w_in, conv_w, conv_b.reshape(1, d_lru), wr, br.reshape(1, d_lru), wi,
      bi.reshape(1, d_lru), lam.reshape(1, d_lru))


_POOL_HALO = 16


def _pool_kernel(v_ref, o_ref, tail_ref):
    t_len, c = v_ref.shape
    gd = c // len(POOL_WINDOWS)
    t = pl.program_id(1)

    @pl.when(t == 0)
    def _():
        tail_ref[...] = jnp.zeros_like(tail_ref)

    v = v_ref[...]
    ext = jnp.concatenate([tail_ref[...], v], axis=0)
    tail_ref[...] = v[t_len - _POOL_HALO:]
    pos1 = t * t_len + lax.broadcasted_iota(jnp.int32, (t_len, 1), 0) + 1
    for g, w in enumerate(POOL_WINDOWS):
        s = ext[:, g * gd:(g + 1) * gd]
        span = 1
        while span < w:
            s = s + pltpu.roll(s, span, 0)
            span *= 2
        count = jnp.minimum(pos1, w).astype(f32)
        o_ref[:, g * gd:(g + 1) * gd] = (
            s[_POOL_HALO:] / count - v[:, g * gd:(g + 1) * gd]).astype(o_ref.dtype)


def _pool_branch(v, bsz, seq):
    n, c = v.shape
    assert max(POOL_WINDOWS) <= _POOL_HALO
    t_len = min(POOL_T, seq)
    nt = seq // t_len
    return pl.pallas_call(
        _pool_kernel,
        grid=(bsz, nt),
        in_specs=[pl.BlockSpec((t_len, c), lambda b, t: (b * nt + t, 0))],
        out_specs=pl.BlockSpec((t_len, c), lambda b, t: (b * nt + t, 0)),
        out_shape=jax.ShapeDtypeStruct((n, c), bf16),
        scratch_shapes=[pltpu.VMEM((_POOL_HALO, c), f32)],
        compiler_params=_cparams("arbitrary", "arbitrary"),
        name="pool",
    )(v)


_HI16 = 0xFFFF0000


def _pack_bf16_pairs(xb):
    half = xb.shape[1] // 2
    bits = lax.bitcast_convert_type(xb.astype(f32), jnp.uint32)
    return (bits[:, :half] >> 16) | (bits[:, half:] & jnp.uint32(_HI16))


def _unpack_bf16_pairs(p):
    lo = lax.bitcast_convert_type(p << 16, f32)
    hi = lax.bitcast_convert_type(p & jnp.uint32(_HI16), f32)
    return jnp.concatenate([lo, hi], axis=1)


def _rows_to_tiles(p):
    return p.reshape(p.shape[0], p.shape[1] // LANES, LANES)


def _tiles_to_rows(t):
    return t.reshape(t.shape[0], t.shape[1] * t.shape[2])


def _router_kernel(x_ref, g_ref, wc_ref, bc_ref, h_ref, ids_ref, wcol_ref, counts_ref, base_ref,
                   w2_ref, *,
                   n_groups, per_group):
    x = x_ref[...]
    ms = jnp.mean(x * x, axis=-1, keepdims=True)
    h = x * lax.rsqrt(ms + NORM_EPS) * g_ref[...]

    hh = h.astype(bf16)
    h_ref[...] = _rows_to_tiles(_pack_bf16_pairs(hh))
    hl = (h - hh.astype(f32)).astype(bf16)

    @pl.when(pl.program_id(0) == 0)
    def _():
        wc = wc_ref[...]
        wh = wc.astype(bf16)
        w2_ref[:, :LANES] = wh
        w2_ref[:, LANES:] = (wc - wh.astype(f32)).astype(bf16)

    hi_part = jnp.dot(hh, w2_ref[...], preferred_element_type=f32)
    logits = (hi_part[:, :LANES] + hi_part[:, LANES:]
              + jnp.dot(hl, w2_ref[:, :LANES], preferred_element_type=f32))
    lt = logits.T + bc_ref[...]

    tm = x.shape[0]
    row = lax.broadcasted_iota(jnp.int32, (per_group, tm), 0)

    def softmax_rows(z):
        e = jnp.exp(z - jnp.max(z, axis=0, keepdims=True))
        return e / jnp.sum(e, axis=0, keepdims=True)

    def top1(p):
        pmax = jnp.max(p, axis=0, keepdims=True)
        idx = jnp.min(jnp.where(p == pmax, row, per_group), axis=0, keepdims=True)
        return pmax, idx

    assert n_groups == per_group == SUBLANES
    g_p, g_idx = top1(softmax_rows(lt[0:n_groups]))
    sel = jnp.zeros((per_group, tm), f32)
    for g in range(n_groups):
        lo = n_groups + g * per_group
        sel = jnp.where(g_idx == g, lt[lo:lo + per_group], sel)
    p = softmax_rows(sel)
    p1, i1 = top1(p)
    p2, i2 = top1(jnp.where(row == i1, -1.0, p))
    den = p1 + p2
    w1 = g_p * p1 / den
    w2 = g_p * p2 / den
    id1 = g_idx * per_group + i1
    id2 = g_idx * per_group + i2
    lane_row = lax.broadcasted_iota(jnp.int32, (LANES, tm), 0)
    wcol_ref[...] = jnp.where(lane_row == 0, w1, jnp.where(lane_row == 1, w2, 0.0)).T

    @pl.when(pl.program_id(0) == 0)
    def _():
        base_ref[...] = jnp.zeros_like(base_ref)

    n_exp = n_groups * per_group
    e_iota = lax.broadcasted_iota(jnp.int32, (n_exp, tm), 0)
    hot1 = e_iota == id1
    hot2 = e_iota == id2
    hot = jnp.concatenate([hot1, hot2], axis=0).astype(f32).astype(bf16)
    src = lax.broadcasted_iota(jnp.int32, (tm, 2 * tm), 0)
    dst = lax.broadcasted_iota(jnp.int32, (tm, 2 * tm), 1)
    tri_ones = ((src < dst) | (dst >= tm)).astype(f32).astype(bf16)
    cnt = jnp.dot(hot, tri_ones, preferred_element_type=f32)
    base = base_ref[...]
    before1 = cnt[:n_exp, :tm] + base
    tot1 = cnt[:n_exp, tm:]
    before2 = cnt[n_exp:, :tm] + base + tot1
    rank1 = jnp.sum(jnp.where(hot1, before1, 0.0), axis=0, keepdims=True).astype(jnp.int32)
    rank2 = jnp.sum(jnp.where(hot2, before2, 0.0), axis=0, keepdims=True).astype(jnp.int32)
    base = base + tot1 + cnt[n_exp:, tm:]
    base_ref[...] = base
    counts_ref[...] = base
    ids_ref[...] = jnp.where(row == 0, id1, jnp.where(row == 1, id2,
                             jnp.where(row == 2, rank1, jnp.where(row == 3, rank2, 0))))


def _router(x1, g, wg, bg, we, be):
    n, d = x1.shape
    n_groups = wg.shape[1]
    per_group = we.shape[1] // n_groups
    n_exp = we.shape[1]
    ncat = n_groups + n_exp
    assert ncat <= LANES and n_exp % SUBLANES == 0
    wc = jnp.pad(jnp.concatenate([wg, we], axis=1), ((0, 0), (0, LANES - ncat)))
    bc = jnp.pad(jnp.concatenate([bg, be]), (0, LANES - ncat)).reshape(LANES, 1)
    tm = min(ROUTER_TM, n)
    return pl.pallas_call(
        functools.partial(_router_kernel, n_groups=n_groups, per_group=per_group),
        grid=(n // tm,),
        in_specs=[pl.BlockSpec((tm, d), lambda i: (i, 0)),
                  pl.BlockSpec((1, d), lambda i: (0, 0)),
                  pl.BlockSpec((d, LANES), lambda i: (0, 0)),
                  pl.BlockSpec((LANES, 1), lambda i: (0, 0))],
        out_specs=[pl.BlockSpec((tm, d // 2 // LANES, LANES), lambda i: (i, 0, 0)),
                   pl.BlockSpec((SUBLANES, tm), lambda i: (0, i)),
                   pl.BlockSpec((tm, LANES), lambda i: (i, 0)),
                   pl.BlockSpec((n_exp, tm), lambda i: (0, 0))],
        out_shape=[jax.ShapeDtypeStruct((n, d // 2 // LANES, LANES), jnp.uint32),
                   jax.ShapeDtypeStruct((SUBLANES, n), jnp.int32),
                   jax.ShapeDtypeStruct((n, LANES), f32),
                   jax.ShapeDtypeStruct((n_exp, tm), f32)],
        scratch_shapes=[pltpu.VMEM((n_exp, tm), f32), pltpu.VMEM((d, 2 * LANES), bf16)],
        compiler_params=_cparams("arbitrary"),
        name="norm_router",
    )(x1, g.reshape(1, d), wc, bc)


def _routing_tables(counts, blk, n_blocks):
    n_experts = counts.shape[0]
    padded = (counts + blk - 1) // blk * blk
    e = jnp.arange(n_experts, dtype=jnp.int32)
    padded_ends = jnp.sum(jnp.where(e[None, :] <= e[:, None], padded[None, :], 0), axis=1)
    padded_offsets = padded_ends - padded
    block_start = jnp.arange(n_blocks, dtype=jnp.int32) * blk
    block_expert = jnp.minimum(
        jnp.sum((padded_ends[None, :] <= block_start[:, None]).astype(jnp.int32), axis=1),
        n_experts - 1)
    n_used = (padded_ends[n_experts - 1:] // blk).astype(jnp.int32)
    b = jnp.arange(n_blocks, dtype=jnp.int32)
    used = b < n_used[0]
    prev_expert = jnp.concatenate([jnp.full((1,), -1, jnp.int32), block_expert[:-1]])
    first = used & (block_expert != prev_expert)
    slot = (jnp.cumsum(first.astype(jnp.int32)) - 1) % 2
    first_pos = jnp.where(first, b, n_blocks)
    next_first = lax.cummin(jnp.concatenate([first_pos[1:], jnp.full((1,), n_blocks, jnp.int32)]),
                            axis=0, reverse=True)
    next_expert = jnp.where(next_first < n_blocks,
                            block_expert[jnp.minimum(next_first, n_blocks - 1)], -1)
    schedule = jnp.stack([block_expert, slot, first.astype(jnp.int32), next_expert]).astype(jnp.int32)
    return padded_offsets, padded_offsets + counts, padded - counts, schedule, n_used


def _tile_index_table(ids, tm):
    n = ids.shape[1]
    rows = 2 * TOP_K
    return ids[:rows].reshape(rows, n // tm, tm).transpose(1, 0, 2).reshape(n // tm, 1, rows * tm)


_DMA_UNROLL = 8


def _dispatch_kernel(off_ref, pad_start_ref, pad_count_ref, idr_ref, h_ref, xs_hbm, zero_ref, sem,
                     zsem):
    tm = h_ref.shape[0]

    def issue(g, carry):
        for u in range(_DMA_UNROLL):
            r = g * _DMA_UNROLL + u
            for k in range(TOP_K):
                dst = off_ref[idr_ref[0, 0, k * tm + r]] + idr_ref[0, 0, (TOP_K + k) * tm + r]
                pltpu.make_async_copy(h_ref.at[pl.ds(r, 1)], xs_hbm.at[pl.ds(dst, 1)],
                                      sem.at[0]).start(priority=k % 2)
        return carry
    lax.fori_loop(0, tm // _DMA_UNROLL, issue, 0)

    @pl.when(pl.program_id(0) == 0)
    def _():
        zero_ref[...] = jnp.zeros_like(zero_ref)

        def pad_copy(e, j):
            return pltpu.make_async_copy(zero_ref.at[pl.ds(0, 1)],
                                         xs_hbm.at[pl.ds(pad_start_ref[e] + j, 1)], zsem.at[0])

        def start_expert(e, carry):
            def start_row(j, c):
                pad_copy(e, j).start()
                return c
            return lax.fori_loop(0, pad_count_ref[e], start_row, carry)

        def wait_expert(e, carry):
            def wait_row(j, c):
                pad_copy(e, j).wait()
                return c
            return lax.fori_loop(0, pad_count_ref[e], wait_row, carry)

        n_experts = pad_start_ref.shape[0]
        lax.fori_loop(0, n_experts, start_expert, 0)
        lax.fori_loop(0, n_experts, wait_expert, 0)

    for _ in range(TOP_K):
        pltpu.make_async_copy(h_ref, xs_hbm.at[pl.ds(0, tm)], sem.at[0]).wait()


def _dispatch(h_packed, ids, padded_offsets, pad_start, pad_count, max_rows):
    n, row_tiles, lanes = h_packed.shape
    tm = min(DISPATCH_TM, n)
    idr = _tile_index_table(ids, tm)
    grid_spec = pltpu.PrefetchScalarGridSpec(
        num_scalar_prefetch=3,
        grid=(n // tm,),
        in_specs=[pl.BlockSpec((1, 1, 2 * TOP_K * tm), lambda i, *_: (i, 0, 0),
                               memory_space=pltpu.SMEM),
                  pl.BlockSpec((tm, row_tiles, lanes), lambda i, *_: (i, 0, 0))],
        out_specs=pl.BlockSpec(memory_space=pl.ANY),
        scratch_shapes=[pltpu.VMEM((1, row_tiles, lanes), jnp.uint32), pltpu.SemaphoreType.DMA((1,)),
                        pltpu.SemaphoreType.DMA((1,))],
    )
    return pl.pallas_call(
        _dispatch_kernel,
        grid_spec=grid_spec,
        out_shape=jax.ShapeDtypeStruct((max_rows, row_tiles, lanes), jnp.uint32),
        compiler_params=_cparams("arbitrary"),
        name="dispatch",
    )(padded_offsets, pad_start, pad_count, idr, h_packed)


def _moe_kernel(bexp_ref, slot_ref, first_ref, next_ref, nused_ref, x_ref, wgu_hbm, wdn_hbm, o_ref,
                wgu_buf, wdn_buf, sem):
    b = pl.program_id(0)
    n_used = nused_ref[0]

    def weight_copies(e, s):
        return (pltpu.make_async_copy(wgu_hbm.at[e], wgu_buf.at[s], sem.at[s, 0]),
                pltpu.make_async_copy(wdn_hbm.at[e], wdn_buf.at[s], sem.at[s, 1]))

    @pl.when(b == 0)
    def _():
        for c in weight_copies(bexp_ref[0], 0):
            c.start()

    @pl.when(first_ref[b] == 1)
    def _():
        for c in weight_copies(bexp_ref[b], slot_ref[b]):
            c.wait()

        @pl.when(next_ref[b] >= 0)
        def _():
            for c in weight_copies(next_ref[b], 1 - slot_ref[b]):
                c.start()

    @pl.when(b < n_used)
    def _():
        s = slot_ref[b]
        x = _unpack_bf16_pairs(_tiles_to_rows(x_ref[...])).astype(bf16)
        gu = jnp.dot(x, wgu_buf[s].astype(bf16), preferred_element_type=f32)
        f = gu.shape[1] // 2
        act = (jax.nn.silu(gu[:, :f]) * gu[:, f:]).astype(bf16)
        y = jnp.dot(act, wdn_buf[s].astype(bf16), preferred_element_type=f32)
        o_ref[...] = _rows_to_tiles(_pack_bf16_pairs(y.astype(bf16)))

    @pl.when(b >= n_used)
    def _():
        o_ref[...] = jnp.zeros_like(o_ref)


def _experts(x_sorted, w_gu, w_down, schedule, n_used, blk):
    max_rows, row_tiles, lanes = x_sorted.shape
    _, d, f2 = w_gu.shape
    assert d == 2 * row_tiles * lanes
    nb = max_rows // blk
    row_block = (blk, row_tiles, lanes)
    grid_spec = pltpu.PrefetchScalarGridSpec(
        num_scalar_prefetch=5,
        grid=(nb,),
        in_specs=[
            pl.BlockSpec(row_block, lambda b, be, sl, fi, nx, nu: (jnp.minimum(b, nu[0] - 1), 0, 0)),
            pl.BlockSpec(memory_space=pl.ANY),
            pl.BlockSpec(memory_space=pl.ANY),
        ],
        out_specs=pl.BlockSpec(row_block, lambda b, be, sl, fi, nx, nu: (jnp.minimum(b, nu[0]), 0, 0)),
        scratch_shapes=[pltpu.VMEM((2, d, f2), f32), pltpu.VMEM((2, f2 // 2, d), f32),
                        pltpu.SemaphoreType.DMA((2, 2))],
    )
    return pl.pallas_call(
        _moe_kernel,
        grid_spec=grid_spec,
        out_shape=jax.ShapeDtypeStruct(x_sorted.shape, jnp.uint32),
        compiler_params=pltpu.CompilerParams(dimension_semantics=("arbitrary",),
                                             vmem_limit_bytes=EXPERTS_VMEM_LIMIT_BYTES),
        name="experts",
    )(schedule[0], schedule[1], schedule[2], schedule[3], n_used, x_sorted, w_gu, w_down)


def _start_expert_row_copy(off_ref, idr_ref, tm, r, y_hbm, buf, sem, slot):
    for k in range(TOP_K):
        src = off_ref[idr_ref[0, 0, k * tm + r]] + idr_ref[0, 0, (TOP_K + k) * tm + r]
        pltpu.make_async_copy(y_hbm.at[pl.ds(src, 1)], buf.at[slot, pl.ds(k * tm + r, 1)],
                              sem.at[slot]).start(priority=k % 2)


def _combine_kernel(off_ref, idr_cur_ref, idr_nxt_ref, x_ref, g_ref, w_ref, y_hbm, o_ref, buf, sem):
    i = pl.program_id(0)
    last = pl.num_programs(0) - 1
    tm = x_ref.shape[0]
    slot = i % 2

    def wait_slot(s):
        pltpu.make_async_copy(y_hbm.at[pl.ds(0, buf.shape[1])], buf.at[s], sem.at[s]).wait()

    @pl.when(i == 0)
    def _():
        def issue(g, carry):
            for u in range(_DMA_UNROLL):
                _start_expert_row_copy(off_ref, idr_cur_ref, tm, g * _DMA_UNROLL + u, y_hbm, buf, sem, 0)
            return carry
        lax.fori_loop(0, tm // _DMA_UNROLL, issue, 0)

    wait_slot(slot)
    rc = tm // COMBINE_PIECES
    for c in range(COMBINE_PIECES):
        for r in range(c * rc, (c + 1) * rc):
            _start_expert_row_copy(off_ref, idr_nxt_ref, tm, r, y_hbm, buf, sem, 1 - slot)
        rows = slice(c * rc, (c + 1) * rc)
        w = w_ref[rows, :]
        y = (w[:, 0:1] * _unpack_bf16_pairs(_tiles_to_rows(buf[slot, c * rc:(c + 1) * rc]))
             + w[:, 1:2] * _unpack_bf16_pairs(_tiles_to_rows(buf[slot, tm + c * rc:tm + (c + 1) * rc])))
        x = x_ref[rows, :] + y
        ms = jnp.mean(x * x, axis=-1, keepdims=True)
        o_ref[rows, :] = x * lax.rsqrt(ms + NORM_EPS) * g_ref[...]

    @pl.when(i == last)
    def _():
        wait_slot(1 - slot)


def _combine(x1, g, y_rows, ids, wcol, padded_offsets):
    n, d = x1.shape
    _, row_tiles, lanes = y_rows.shape
    tm = min(COMBINE_TM, n)
    nt = n // tm
    idr = _tile_index_table(ids, tm)
    grid_spec = pltpu.PrefetchScalarGridSpec(
        num_scalar_prefetch=1,
        grid=(nt,),
        in_specs=[
            pl.BlockSpec((1, 1, 2 * TOP_K * tm), lambda i, off: (i, 0, 0), memory_space=pltpu.SMEM),
            pl.BlockSpec((1, 1, 2 * TOP_K * tm), lambda i, off: (jnp.minimum(i + 1, nt - 1), 0, 0),
                         memory_space=pltpu.SMEM),
            pl.BlockSpec((tm, d), lambda i, off: (i, 0)),
            pl.BlockSpec((1, d), lambda i, off: (0, 0)),
            pl.BlockSpec((tm, LANES), lambda i, off: (i, 0)),
            pl.BlockSpec(memory_space=pl.ANY),
        ],
        out_specs=pl.BlockSpec((tm, d), lambda i, off: (i, 0)),
        scratch_shapes=[pltpu.VMEM((2, TOP_K * tm, row_tiles, lanes), jnp.uint32),
                        pltpu.SemaphoreType.DMA((2,))],
    )
    return pl.pallas_call(
        _combine_kernel,
        grid_spec=grid_spec,
        out_shape=jax.ShapeDtypeStruct((n, d), f32),
        compiler_params=_cparams("arbitrary"),
        name="combine_norm",
    )(padded_offsets, idr, idr, x1, g.reshape(1, d), wcol, y_rows)


def _layer(x2, bsz, seq, norm_mix_g, w_in, b_gate, conv_w, conv_b, lru_wr, lru_br, lru_wi, lru_bi,
           lru_lambda, w_lru_out, pool_w, pool_scale, w_out, norm_ffn_g, router_wg, router_bg,
           router_we, router_be, exp_w_gu, exp_w_down):
    n, d = x2.shape
    d_lru = w_lru_out.shape[0]
    n_pool_groups, pool_gd, pool_od = pool_w.shape
    d_pool = n_pool_groups * pool_gd
    col_gate, col_pool, col_merge = d_lru, 2 * d_lru, 2 * d_lru + d_pool
    tn = min(MM_TN, pool_od, d_pool)
    tm = min(MM_TM, n)

    xn = _rmsnorm(x2, norm_mix_g, bf16)
    hg = _inproj_lru(xn, w_in, col_gate, conv_w, conv_b, lru_wr, lru_br, lru_wi, lru_bi, lru_lambda,
                     seq)
    v = _matmul(xn, w_in, col_pool, d_pool, f32, lambda acc: acc, tm, tn, name="in_proj_pool")
    mg = _matmul(xn, w_in, col_merge, 2 * d, bf16,
                 lambda acc, b: jax.nn.sigmoid(acc + b), tm, tn,
                 extras=(b_gate.reshape(1, 2 * d),),
                 extra_specs=(pl.BlockSpec((1, tn), lambda j, i: (0, j)),),
                 name="in_proj_merge")

    pooled = _pool_branch(v, bsz, seq)

    assert pool_od % tn == 0
    per_g = pool_od // tn

    def merge_epilogue(acc, pooled_t, pw, ps, mga, mgb):
        yb = jnp.dot(pooled_t, pw[0].astype(bf16), preferred_element_type=f32) * ps
        return mga.astype(f32) * acc + mgb.astype(f32) * yb

    merged = _matmul(
        hg, w_lru_out, 0, d, bf16, merge_epilogue, tm, tn,
        extras=(pooled, pool_w, pool_scale.reshape(1, d), mg, mg),
        extra_specs=(pl.BlockSpec((tm, pool_gd), lambda j, i: (i, j // per_g)),
                     pl.BlockSpec((1, pool_gd, tn), lambda j, i: (j // per_g, 0, j % per_g)),
                     pl.BlockSpec((1, tn), lambda j, i: (0, j)),
                     pl.BlockSpec((tm, tn), lambda j, i: (i, j)),
                     pl.BlockSpec((tm, tn), lambda j, i: (i, j + d // tn))),
        name="lru_out_merge")

    x1 = _matmul(merged, w_out, 0, d, f32, lambda acc, xr: xr + acc, tm, tn,
                 extras=(x2,), extra_specs=(pl.BlockSpec((tm, tn), lambda j, i: (i, j)),),
                 name="out_proj")

    h2, ids, wcol, counts_f = _router(x1, norm_ffn_g, router_wg, router_bg, router_we, router_be)
    n_experts = exp_w_gu.shape[0]
    max_rows = n * TOP_K + n_experts * EXPERT_BLOCK
    offsets, pad_start, pad_count, schedule, n_used = _routing_tables(
        counts_f[:, 0].astype(jnp.int32), EXPERT_BLOCK, max_rows // EXPERT_BLOCK)
    x_sorted = _dispatch(h2, ids, offsets, pad_start, pad_count, max_rows)
    y_rows = _experts(x_sorted, exp_w_gu, exp_w_down, schedule, n_used, EXPERT_BLOCK)
    return x1, y_rows, ids, wcol, offsets


def kernel(x, norm_mix_g, w_in, b_gate, conv_w, conv_b, lru_wr, lru_br, lru_wi, lru_bi, lru_lambda,
           w_lru_out, pool_w, pool_scale, w_out, norm_ffn_g, router_wg, router_bg, router_we,
           router_be, exp_w_gu, exp_w_down, norm_final_g):
    bsz, seq, d = x.shape
    depth = w_in.shape[0]
    assert depth == 1
    x2 = x.reshape(bsz * seq, d)
    x1, y_rows, ids, wcol, offsets = _layer(
        x2, bsz, seq, norm_mix_g[0], w_in[0], b_gate[0], conv_w[0], conv_b[0], lru_wr[0], lru_br[0],
        lru_wi[0], lru_bi[0], lru_lambda[0], w_lru_out[0], pool_w[0], pool_scale[0], w_out[0],
        norm_ffn_g[0], router_wg[0], router_bg[0], router_we[0], router_be[0], exp_w_gu[0],
        exp_w_down[0])
    out = _combine(x1, norm_final_g, y_rows, ids, wcol, offsets)
    return out.reshape(bsz, seq, d)
```

```python
import functools

import jax
import jax.numpy as jnp
from jax import lax
from jax.experimental import pallas as pl
from jax.experimental.pallas import tpu as pltpu

NORM_EPS = 1e-6
LRU_C = 8.0
POOL_WINDOWS = (2, 4, 8, 16)
TOP_K = 2

VMEM_LIMIT_BYTES = 56 * 1024 * 1024
SUBLANES = 8
LANES = 128

MM_TM = 512
MM_TN = 1024
NORM_TM = 512
LRU_TM = 1024
LRU_PARTS = 4
POOL_T = 512
ROUTER_TM = 256
EXPERT_BLOCK = 256
DISPATCH_TM = 512
EXPERTS_VMEM_LIMIT_BYTES = 63 * 1024 * 1024
COMBINE_TM = 256
COMBINE_PIECES = 8

f32 = jnp.float32
bf16 = jnp.bfloat16


def _cparams(*sem):
    return pltpu.CompilerParams(dimension_semantics=sem, vmem_limit_bytes=VMEM_LIMIT_BYTES)


def _rmsnorm_kernel(x_ref, g_ref, o_ref):
    x = x_ref[...]
    ms = jnp.mean(x * x, axis=-1, keepdims=True)
    o_ref[...] = (x * lax.rsqrt(ms + NORM_EPS) * g_ref[...]).astype(o_ref.dtype)


def _rmsnorm(x, g, out_dtype):
    n, d = x.shape
    return pl.pallas_call(
        _rmsnorm_kernel,
        grid=(n // NORM_TM,),
        in_specs=[pl.BlockSpec((NORM_TM, d), lambda i: (i, 0)),
                  pl.BlockSpec((1, d), lambda i: (0, 0))],
        out_specs=pl.BlockSpec((NORM_TM, d), lambda i: (i, 0)),
        out_shape=jax.ShapeDtypeStruct((n, d), out_dtype),
        compiler_params=_cparams("arbitrary"),
        name="rmsnorm",
    )(x, g.reshape(1, d))


def _mm_kernel(*refs, n_extra, epilogue):
    lhs_ref, w_ref = refs[0], refs[1]
    extras = refs[2:2 + n_extra]
    o_ref = refs[2 + n_extra]
    wbf_ref = refs[3 + n_extra]

    @pl.when(pl.program_id(1) == 0)
    def _():
        wbf_ref[...] = w_ref[...].astype(bf16)

    acc = jnp.dot(lhs_ref[...], wbf_ref[...], preferred_element_type=f32)
    o_ref[...] = epilogue(acc, *[e[...] for e in extras]).astype(o_ref.dtype)


def _matmul(lhs, w, col_off, ncols, out_dtype, epilogue, tm, tn, extras=(), extra_specs=(),
            name="mm"):
    m, k = lhs.shape
    assert m % tm == 0 and ncols % tn == 0 and col_off % tn == 0
    off = col_off // tn
    return pl.pallas_call(
        functools.partial(_mm_kernel, n_extra=len(extras), epilogue=epilogue),
        grid=(ncols // tn, m // tm),
        in_specs=[pl.BlockSpec((tm, k), lambda j, i: (i, 0)),
                  pl.BlockSpec((k, tn), lambda j, i: (0, j + off), pipeline_mode=pl.Buffered(1))]
        + list(extra_specs),
        out_specs=pl.BlockSpec((tm, tn), lambda j, i: (i, j)),
        out_shape=jax.ShapeDtypeStruct((m, ncols), out_dtype),
        scratch_shapes=[pltpu.VMEM((k, tn), bf16)],
        compiler_params=_cparams("arbitrary", "arbitrary"),
        name=name,
    )(lhs, w, *extras)


def _lru_part(acc, tails, carry, cw, cb, wr, br, wi, bi, c_sp, u_scr, g_scr, o_scr):
    rows, c = acc.shape[0], acc.shape[1] // 2
    groups, kw, n_slabs = rows // SUBLANES, cw.shape[0], c // LANES
    grow = lax.broadcasted_iota(jnp.int32, (groups, LANES), 0)
    sub3 = lax.broadcasted_iota(jnp.int32, (groups // SUBLANES, SUBLANES, LANES), 1)
    sub8 = lax.broadcasted_iota(jnp.int32, (SUBLANES, LANES), 0)

    def piece(scr, s, j):
        return scr[s, pl.ds(j, groups, stride=SUBLANES), :]

    uc_pieces, new_tails = [], {j: [] for j in tails}
    for s in range(n_slabs):
        lanes = slice(s * LANES, (s + 1) * LANES)
        u_scr[s] = acc[:, lanes]
        g_scr[s] = acc[:, c + s * LANES:c + (s + 1) * LANES]
        u = [piece(u_scr, s, j) for j in range(SUBLANES)]
        before = {}
        for j in tails:
            before[j] = jnp.where(grow == 0, tails[j][:, lanes], pltpu.roll(u[j], 1, 0))
            new_tails[j].append(u[j][groups - 1:groups])
        ucs = []
        for j in range(SUBLANES):
            v = u[j] * cw[kw - 1:kw, lanes] + cb[:, lanes]
            for k in range(1, kw):
                src = u[j - k] if j >= k else before[j - k + SUBLANES]
                v = v + src * cw[kw - 1 - k:kw - k, lanes]
            ucs.append(v)
        uc_pieces.append(jnp.concatenate(ucs, axis=0))
    uc = jnp.concatenate(uc_pieces, axis=1)

    ub = uc.astype(bf16)
    r_logit = jnp.dot(ub, wr, preferred_element_type=f32) + br
    i_logit = jnp.dot(ub, wi, preferred_element_type=f32) + bi
    neg_log_a = jax.nn.sigmoid(r_logit) * c_sp
    a = jnp.exp(-neg_log_a)
    z = jnp.tanh(neg_log_a) * (1.0 + a * a)
    beta = jnp.where(z > 0.0, z * lax.rsqrt(z), 0.0)
    b = beta * (jax.nn.sigmoid(i_logit) * uc)

    carries = []
    for s in range(n_slabs):
        lanes = slice(s * LANES, (s + 1) * LANES)
        ps, hs = [], []
        for j in range(SUBLANES):
            aj = a[j * groups:(j + 1) * groups, lanes]
            bj = b[j * groups:(j + 1) * groups, lanes]
            hs.append(bj if j == 0 else aj * hs[-1] + bj)
            ps.append(aj if j == 0 else aj * ps[-1])
        pt = ps[-1].reshape(groups // SUBLANES, SUBLANES, LANES)
        ht = hs[-1].reshape(groups // SUBLANES, SUBLANES, LANES)
        d = 1
        while d < SUBLANES:
            keep = sub3 >= d
            ht = jnp.where(keep, pt * pltpu.roll(ht, d, 1) + ht, ht)
            pt = jnp.where(keep, pt * pltpu.roll(pt, d, 1), pt)
            d *= 2
        state = carry[:, lanes]
        incoming = []
        for q in range(groups // SUBLANES):
            after = ht[q] + pt[q] * state
            incoming.append(jnp.where(sub8 == 0, state, pltpu.roll(after, 1, 0)))
            state = after[SUBLANES - 1:SUBLANES]
        carries.append(state)
        incoming = jnp.concatenate(incoming, axis=0)
        for j in range(SUBLANES):
            h = hs[j] + ps[j] * incoming
            o_scr[s, pl.ds(j, groups, stride=SUBLANES), :] = h * jax.nn.gelu(piece(g_scr, s, j))
    out = jnp.concatenate([o_scr[s] for s in range(n_slabs)], axis=1)
    new_tails = {j: jnp.concatenate(v, axis=1) for j, v in new_tails.items()}
    return out, new_tails, jnp.concatenate(carries, axis=1)


def _inproj_lru_kernel(xn_ref, wu_ref, wg_ref, cw_ref, cb_ref, wr_ref, br_ref, wi_ref, bi_ref,
                       lam_ref, o_ref, wbf_ref, tail_ref, carry_ref, u_scr, g_scr, o_scr, *,
                       tiles_per_seq, parts):
    hd = wu_ref.shape[1]
    i = pl.program_id(1)

    @pl.when(i == 0)
    def _():
        wbf_ref[:, :hd] = wu_ref[...].astype(bf16)
        wbf_ref[:, hd:] = wg_ref[...].astype(bf16)

    @pl.when(i % tiles_per_seq == 0)
    def _():
        tail_ref[...] = jnp.zeros_like(tail_ref)
        carry_ref[...] = jnp.zeros_like(carry_ref)

    cw, cb = cw_ref[...], cb_ref[...]
    wr, wi = wr_ref[0].astype(bf16), wi_ref[0].astype(bf16)
    br, bi = br_ref[...], bi_ref[...]
    c_sp = LRU_C * jax.nn.softplus(-lam_ref[...])
    kw = cw.shape[0]
    tails = {j: tail_ref[j:j + 1, :] for j in range(SUBLANES - kw + 1, SUBLANES)}
    carry = carry_ref[...]
    rows = xn_ref.shape[0] // parts
    accs = [jnp.dot(xn_ref[p * rows:(p + 1) * rows, :], wbf_ref[...], preferred_element_type=f32)
            for p in range(parts)]
    for p in range(parts):
        hg, tails, carry = _lru_part(accs[p], tails, carry, cw, cb, wr, br, wi, bi, c_sp,
                                     u_scr, g_scr, o_scr)
        o_ref[p * rows:(p + 1) * rows, :] = hg.astype(o_ref.dtype)
    for j, row in tails.items():
        tail_ref[j:j + 1, :] = row
    carry_ref[...] = carry


def _inproj_lru(xn, w_in, col_gate, conv_w, conv_b, wr, br, wi, bi, lam, seq):
    n, k = xn.shape
    heads, hd, _ = wr.shape
    d_lru = heads * hd
    kw = conv_w.shape[0]
    tm = min(LRU_TM, seq)
    assert seq % tm == 0 and col_gate % hd == 0 and (tm // LRU_PARTS) % SUBLANES == 0
    gate_off = col_gate // hd
    vec = lambda h, i: (0, h)
    return pl.pallas_call(
        functools.partial(_inproj_lru_kernel, tiles_per_seq=seq // tm, parts=LRU_PARTS),
        grid=(heads, n // tm),
        in_specs=[pl.BlockSpec((tm, k), lambda h, i: (i, 0)),
                  pl.BlockSpec((k, hd), lambda h, i: (0, h)),
                  pl.BlockSpec((k, hd), lambda h, i: (0, gate_off + h)),
                  pl.BlockSpec((kw, hd), vec),
                  pl.BlockSpec((1, hd), vec),
                  pl.BlockSpec((1, hd, hd), lambda h, i: (h, 0, 0)),
                  pl.BlockSpec((1, hd), vec),
                  pl.BlockSpec((1, hd, hd), lambda h, i: (h, 0, 0)),
                  pl.BlockSpec((1, hd), vec),
                  pl.BlockSpec((1, hd), vec)],
        out_specs=pl.BlockSpec((tm, hd), lambda h, i: (i, h)),
        out_shape=jax.ShapeDtypeStruct((n, d_lru), bf16),
        scratch_shapes=[pltpu.VMEM((k, 2 * hd), bf16), pltpu.VMEM((SUBLANES, hd), f32),
                        pltpu.VMEM((1, hd), f32)]
        + [pltpu.VMEM((hd // LANES, tm // LRU_PARTS, LANES), f32)] * 3,
        compiler_params=_cparams("arbitrary", "arbitrary"),
        name="in_proj_rg_lru",
    )(xn, w_in, w_in, conv_w, conv_b.reshape(1, d_lru), wr, br.reshape(1, d_lru), wi,
      bi.reshape(1, d_lru), lam.reshape(1, d_lru))


_POOL_HALO = 16


def _pool_kernel(v_ref, o_ref, tail_ref):
    t_len, c = v_ref.shape
    gd = c // len(POOL_WINDOWS)
    t = pl.program_id(1)

    @pl.when(t == 0)
    def _():
        tail_ref[...] = jnp.zeros_like(tail_ref)

    v = v_ref[...]
    ext = jnp.concatenate([tail_ref[...], v], axis=0)
    tail_ref[...] = v[t_len - _POOL_HALO:]
    pos1 = t * t_len + lax.broadcasted_iota(jnp.int32, (t_len, 1), 0) + 1
    for g, w in enumerate(POOL_WINDOWS):
        s = ext[:, g * gd:(g + 1) * gd]
        span = 1
        while span < w:
            s = s + pltpu.roll(s, span, 0)
            span *= 2
        count = jnp.minimum(pos1, w).astype(f32)
        o_ref[:, g * gd:(g + 1) * gd] = (
            s[_POOL_HALO:] / count - v[:, g * gd:(g + 1) * gd]).astype(o_ref.dtype)


def _pool_branch(v, bsz, seq):
    n, c = v.shape
    assert max(POOL_WINDOWS) <= _POOL_HALO
    t_len = min(POOL_T, seq)
    nt = seq // t_len
    return pl.pallas_call(
        _pool_kernel,
        grid=(bsz, nt),
        in_specs=[pl.BlockSpec((t_len, c), lambda b, t: (b * nt + t, 0))],
        out_specs=pl.BlockSpec((t_len, c), lambda b, t: (b * nt + t, 0)),
        out_shape=jax.ShapeDtypeStruct((n, c), bf16),
        scratch_shapes=[pltpu.VMEM((_POOL_HALO, c), f32)],
        compiler_params=_cparams("arbitrary", "arbitrary"),
        name="pool",
    )(v)


_HI16 = 0xFFFF0000


def _pack_bf16_pairs(xb):
    half = xb.shape[1] // 2
    bits = lax.bitcast_convert_type(xb.astype(f32), jnp.uint32)
    return (bits[:, :half] >> 16) | (bits[:, half:] & jnp.uint32(_HI16))


def _unpack_bf16_pairs(p):
    lo = lax.bitcast_convert_type(p << 16, f32)
    hi = lax.bitcast_convert_type(p & jnp.uint32(_HI16), f32)
    return jnp.concatenate([lo, hi], axis=1)


def _rows_to_tiles(p):
    return p.reshape(p.shape[0], p.shape[1] // LANES, LANES)


def _tiles_to_rows(t):
    return t.reshape(t.shape[0], t.shape[1] * t.shape[2])


def _router_kernel(x_ref, g_ref, wc_ref, bc_ref, h_ref, ids_ref, wcol_ref, counts_ref, base_ref,
                   w2_ref, *,
                   n_groups, per_group):
    x = x_ref[...]
    ms = jnp.mean(x * x, axis=-1, keepdims=True)
    h = x * lax.rsqrt(ms + NORM_EPS) * g_ref[...]

    hh = h.astype(bf16)
    h_ref[...] = _rows_to_tiles(_pack_bf16_pairs(hh))
    hl = (h - hh.astype(f32)).astype(bf16)

    @pl.when(pl.program_id(0) == 0)
    def _():
        wc = wc_ref[...]
        wh = wc.astype(bf16)
        w2_ref[:, :LANES] = wh
        w2_ref[:, LANES:] = (wc - wh.astype(f32)).astype(bf16)

    hi_part = jnp.dot(hh, w2_ref[...], preferred_element_type=f32)
    logits = (hi_part[:, :LANES] + hi_part[:, LANES:]
              + jnp.dot(hl, w2_ref[:, :LANES], preferred_element_type=f32))
    lt = logits.T + bc_ref[...]

    tm = x.shape[0]
    row = lax.broadcasted_iota(jnp.int32, (per_group, tm), 0)

    def softmax_rows(z):
        e = jnp.exp(z - jnp.max(z, axis=0, keepdims=True))
        return e / jnp.sum(e, axis=0, keepdims=True)

    def top1(p):
        pmax = jnp.max(p, axis=0, keepdims=True)
        idx = jnp.min(jnp.where(p == pmax, row, per_group), axis=0, keepdims=True)
        return pmax, idx

    assert n_groups == per_group == SUBLANES
    g_p, g_idx = top1(softmax_rows(lt[0:n_groups]))
    sel = jnp.zeros((per_group, tm), f32)
    for g in range(n_groups):
        lo = n_groups + g * per_group
        sel = jnp.where(g_idx == g, lt[lo:lo + per_group], sel)
    p = softmax_rows(sel)
    p1, i1 = top1(p)
    p2, i2 = top1(jnp.where(row == i1, -1.0, p))
    den = p1 + p2
    w1 = g_p * p1 / den
    w2 = g_p * p2 / den
    id1 = g_idx * per_group + i1
    id2 = g_idx * per_group + i2
    lane_row = lax.broadcasted_iota(jnp.int32, (LANES, tm), 0)
    wcol_ref[...] = jnp.where(lane_row == 0, w1, jnp.where(lane_row == 1, w2, 0.0)).T

    @pl.when(pl.program_id(0) == 0)
    def _():
        base_ref[...] = jnp.zeros_like(base_ref)

    n_exp = n_groups * per_group
    e_iota = lax.broadcasted_iota(jnp.int32, (n_exp, tm), 0)
    hot1 = e_iota == id1
    hot2 = e_iota == id2
    hot = jnp.concatenate([hot1, hot2], axis=0).astype(f32).astype(bf16)
    src = lax.broadcasted_iota(jnp.int32, (tm, 2 * tm), 0)
    dst = lax.broadcasted_iota(jnp.int32, (tm, 2 * tm), 1)
    tri_ones = ((src < dst) | (dst >= tm)).astype(f32).astype(bf16)
    cnt = jnp.dot(hot, tri_ones, preferred_element_type=f32)
    base = base_ref[...]
    before1 = cnt[:n_exp, :tm] + base
    tot1 = cnt[:n_exp, tm:]
    before2 = cnt[n_exp:, :tm] + base + tot1
    rank1 = jnp.sum(jnp.where(hot1, before1, 0.0), axis=0, keepdims=True).astype(jnp.int32)
    rank2 = jnp.sum(jnp.where(hot2, before2, 0.0), axis=0, keepdims=True).astype(jnp.int32)
    base = base + tot1 + cnt[n_exp:, tm:]
    base_ref[...] = base
    counts_ref[...] = base
    ids_ref[...] = jnp.where(row == 0, id1, jnp.where(row == 1, id2,
                             jnp.where(row == 2, rank1, jnp.where(row == 3, rank2, 0))))


def _router(x1, g, wg, bg, we, be):
    n, d = x1.shape
    n_groups = wg.shape[1]
    per_group = we.shape[1] // n_groups
    n_exp = we.shape[1]
    ncat = n_groups + n_exp
    assert ncat <= LANES and n_exp % SUBLANES == 0
    wc = jnp.pad(jnp.concatenate([wg, we], axis=1), ((0, 0), (0, LANES - ncat)))
    bc = jnp.pad(jnp.concatenate([bg, be]), (0, LANES - ncat)).reshape(LANES, 1)
    tm = min(ROUTER_TM, n)
    return pl.pallas_call(
        functools.partial(_router_kernel, n_groups=n_groups, per_group=per_group),
        grid=(n // tm,),
        in_specs=[pl.BlockSpec((tm, d), lambda i: (i, 0)),
                  pl.BlockSpec((1, d), lambda i: (0, 0)),
                  pl.BlockSpec((d, LANES), lambda i: (0, 0)),
                  pl.BlockSpec((LANES, 1), lambda i: (0, 0))],
        out_specs=[pl.BlockSpec((tm, d // 2 // LANES, LANES), lambda i: (i, 0, 0)),
                   pl.BlockSpec((SUBLANES, tm), lambda i: (0, i)),
                   pl.BlockSpec((tm, LANES), lambda i: (i, 0)),
                   pl.BlockSpec((n_exp, tm), lambda i: (0, 0))],
        out_shape=[jax.ShapeDtypeStruct((n, d // 2 // LANES, LANES), jnp.uint32),
                   jax.ShapeDtypeStruct((SUBLANES, n), jnp.int32),
                   jax.ShapeDtypeStruct((n, LANES), f32),
                   jax.ShapeDtypeStruct((n_exp, tm), f32)],
        scratch_shapes=[pltpu.VMEM((n_exp, tm), f32), pltpu.VMEM((d, 2 * LANES), bf16)],
        compiler_params=_cparams("arbitrary"),
        name="norm_router",
    )(x1, g.reshape(1, d), wc, bc)


def _routing_tables(counts, blk, n_blocks):
    n_experts = counts.shape[0]
    padded = (counts + blk - 1) // blk * blk
    e = jnp.arange(n_experts, dtype=jnp.int32)
    padded_ends = jnp.sum(jnp.where(e[None, :] <= e[:, None], padded[None, :], 0), axis=1)
    padded_offsets = padded_ends - padded
    block_start = jnp.arange(n_blocks, dtype=jnp.int32) * blk
    block_expert = jnp.minimum(
        jnp.sum((padded_ends[None, :] <= block_start[:, None]).astype(jnp.int32), axis=1),
        n_experts - 1)
    n_used = (padded_ends[n_experts - 1:] // blk).astype(jnp.int32)
    b = jnp.arange(n_blocks, dtype=jnp.int32)
    used = b < n_used[0]
    prev_expert = jnp.concatenate([jnp.full((1,), -1, jnp.int32), block_expert[:-1]])
    first = used & (block_expert != prev_expert)
    slot = (jnp.cumsum(first.astype(jnp.int32)) - 1) % 2
    first_pos = jnp.where(first, b, n_blocks)
    next_first = lax.cummin(jnp.concatenate([first_pos[1:], jnp.full((1,), n_blocks, jnp.int32)]),
                            axis=0, reverse=True)
    next_expert = jnp.where(next_first < n_blocks,
                            block_expert[jnp.minimum(next_first, n_blocks - 1)], -1)
    schedule = jnp.stack([block_expert, slot, first.astype(jnp.int32), next_expert]).astype(jnp.int32)
    return padded_offsets, padded_offsets + counts, padded - counts, schedule, n_used


def _tile_index_table(ids, tm):
    n = ids.shape[1]
    rows = 2 * TOP_K
    return ids[:rows].reshape(rows, n // tm, tm).transpose(1, 0, 2).reshape(n // tm, 1, rows * tm)


_DMA_UNROLL = 8


def _dispatch_kernel(off_ref, pad_start_ref, pad_count_ref, idr_ref, h_ref, xs_hbm, zero_ref, sem,
                     zsem):
    tm = h_ref.shape[0]

    def issue(g, carry):
        for u in range(_DMA_UNROLL):
            r = g * _DMA_UNROLL + u
            for k in range(TOP_K):
                dst = off_ref[idr_ref[0, 0, k * tm + r]] + idr_ref[0, 0, (TOP_K + k) * tm + r]
                pltpu.make_async_copy(h_ref.at[pl.ds(r, 1)], xs_hbm.at[pl.ds(dst, 1)],
                                      sem.at[0]).start(priority=k % 2)
        return carry
    lax.fori_loop(0, tm // _DMA_UNROLL, issue, 0)

    @pl.when(pl.program_id(0) == 0)
    def _():
        zero_ref[...] = jnp.zeros_like(zero_ref)

        chunk = zero_ref.shape[0]
        chunks = [chunk >> s for s in range(chunk.bit_length())]

        def pad_copies(e):
            count = pad_count_ref[e]
            for c in chunks:
                offset = count & ~(2 * c - 1)
                yield (count & c) != 0, pltpu.make_async_copy(
                    zero_ref.at[pl.ds(0, c)], xs_hbm.at[pl.ds(pad_start_ref[e] + offset, c)],
                    zsem.at[0])

        def start_expert(e, carry):
            for present, copy in pad_copies(e):
                pl.when(present)(copy.start)
            return carry

        def wait_expert(e, carry):
            for present, copy in pad_copies(e):
                pl.when(present)(copy.wait)
            return carry

        n_experts = pad_start_ref.shape[0]
        lax.fori_loop(0, n_experts, start_expert, 0)
        lax.fori_loop(0, n_experts, wait_expert, 0)

    for _ in range(TOP_K):
        pltpu.make_async_copy(h_ref, xs_hbm.at[pl.ds(0, tm)], sem.at[0]).wait()


def _dispatch(h_packed, ids, padded_offsets, pad_start, pad_count, max_rows, blk):
    n, row_tiles, lanes = h_packed.shape
    tm = min(DISPATCH_TM, n)
    idr = _tile_index_table(ids, tm)
    grid_spec = pltpu.PrefetchScalarGridSpec(
        num_scalar_prefetch=3,
        grid=(n // tm,),
        in_specs=[pl.BlockSpec((1, 1, 2 * TOP_K * tm), lambda i, *_: (i, 0, 0),
                               memory_space=pltpu.SMEM),
                  pl.BlockSpec((tm, row_tiles, lanes), lambda i, *_: (i, 0, 0))],
        out_specs=pl.BlockSpec(memory_space=pl.ANY),
        scratch_shapes=[pltpu.VMEM((pl.next_power_of_2(blk) // 2, row_tiles, lanes), jnp.uint32),
                        pltpu.SemaphoreType.DMA((1,)), pltpu.SemaphoreType.DMA((1,))],
    )
    return pl.pallas_call(
        _dispatch_kernel,
        grid_spec=grid_spec,
        out_shape=jax.ShapeDtypeStruct((max_rows, row_tiles, lanes), jnp.uint32),
        compiler_params=_cparams("arbitrary"),
        name="dispatch",
    )(padded_offsets, pad_start, pad_count, idr, h_packed)


def _moe_kernel(bexp_ref, slot_ref, first_ref, next_ref, nused_ref, x_ref, wgu_hbm, wdn_hbm, o_ref,
                wgu_buf, wdn_buf, sem):
    b = pl.program_id(0)
    n_used = nused_ref[0]

    def weight_copies(e, s):
        return (pltpu.make_async_copy(wgu_hbm.at[e], wgu_buf.at[s], sem.at[s, 0]),
                pltpu.make_async_copy(wdn_hbm.at[e], wdn_buf.at[s], sem.at[s, 1]))

    @pl.when(b == 0)
    def _():
        for c in weight_copies(bexp_ref[0], 0):
            c.start()

    @pl.when(first_ref[b] == 1)
    def _():
        for c in weight_copies(bexp_ref[b], slot_ref[b]):
            c.wait()

        @pl.when(next_ref[b] >= 0)
        def _():
            for c in weight_copies(next_ref[b], 1 - slot_ref[b]):
                c.start()

    @pl.when(b < n_used)
    def _():
        s = slot_ref[b]
        x = _unpack_bf16_pairs(_tiles_to_rows(x_ref[...])).astype(bf16)
        gu = jnp.dot(x, wgu_buf[s].astype(bf16), preferred_element_type=f32)
        f = gu.shape[1] // 2
        act = (jax.nn.silu(gu[:, :f]) * gu[:, f:]).astype(bf16)
        y = jnp.dot(act, wdn_buf[s].astype(bf16), preferred_element_type=f32)
        o_ref[...] = _rows_to_tiles(_pack_bf16_pairs(y.astype(bf16)))

    @pl.when(b >= n_used)
    def _():
        o_ref[...] = jnp.zeros_like(o_ref)


def _experts(x_sorted, w_gu, w_down, schedule, n_used, blk):
    max_rows, row_tiles, lanes = x_sorted.shape
    _, d, f2 = w_gu.shape
    assert d == 2 * row_tiles * lanes
    nb = max_rows // blk
    row_block = (blk, row_tiles, lanes)
    grid_spec = pltpu.PrefetchScalarGridSpec(
        num_scalar_prefetch=5,
        grid=(nb,),
        in_specs=[
            pl.BlockSpec(row_block, lambda b, be, sl, fi, nx, nu: (jnp.minimum(b, nu[0] - 1), 0, 0)),
            pl.BlockSpec(memory_space=pl.ANY),
            pl.BlockSpec(memory_space=pl.ANY),
        ],
        out_specs=pl.BlockSpec(row_block, lambda b, be, sl, fi, nx, nu: (jnp.minimum(b, nu[0]), 0, 0)),
        scratch_shapes=[pltpu.VMEM((2, d, f2), f32), pltpu.VMEM((2, f2 // 2, d), f32),
                        pltpu.SemaphoreType.DMA((2, 2))],
    )
    return pl.pallas_call(
        _moe_kernel,
        grid_spec=grid_spec,
        out_shape=jax.ShapeDtypeStruct(x_sorted.shape, jnp.uint32),
        compiler_params=pltpu.CompilerParams(dimension_semantics=("arbitrary",),
                                             vmem_limit_bytes=EXPERTS_VMEM_LIMIT_BYTES),
        name="experts",
    )(schedule[0], schedule[1], schedule[2], schedule[3], n_used, x_sorted, w_gu, w_down)


def _start_expert_row_copy(off_ref, idr_ref, tm, r, y_hbm, buf, sem, slot):
    for k in range(TOP_K):
        src = off_ref[idr_ref[0, 0, k * tm + r]] + idr_ref[0, 0, (TOP_K + k) * tm + r]
        pltpu.make_async_copy(y_hbm.at[pl.ds(src, 1)], buf.at[slot, pl.ds(k * tm + r, 1)],
                              sem.at[slot]).start(priority=k % 2)


def _combine_kernel(off_ref, idr_cur_ref, idr_nxt_ref, x_ref, g_ref, w_ref, y_hbm, o_ref, buf, sem):
    i = pl.program_id(0)
    last = pl.num_programs(0) - 1
    tm = x_ref.shape[0]
    slot = i % 2

    def wait_slot(s):
        pltpu.make_async_copy(y_hbm.at[pl.ds(0, buf.shape[1])], buf.at[s], sem.at[s]).wait()

    @pl.when(i == 0)
    def _():
        def issue(g, carry):
            for u in range(_DMA_UNROLL):
                _start_expert_row_copy(off_ref, idr_cur_ref, tm, g * _DMA_UNROLL + u, y_hbm, buf, sem, 0)
            return carry
        lax.fori_loop(0, tm // _DMA_UNROLL, issue, 0)

    wait_slot(slot)
    rc = tm // COMBINE_PIECES
    for c in range(COMBINE_PIECES):
        for r in range(c * rc, (c + 1) * rc):
            _start_expert_row_copy(off_ref, idr_nxt_ref, tm, r, y_hbm, buf, sem, 1 - slot)
        rows = slice(c * rc, (c + 1) * rc)
        w = w_ref[rows, :]
        y = (w[:, 0:1] * _unpack_bf16_pairs(_tiles_to_rows(buf[slot, c * rc:(c + 1) * rc]))
             + w[:, 1:2] * _unpack_bf16_pairs(_tiles_to_rows(buf[slot, tm + c * rc:tm + (c + 1) * rc])))
        x = x_ref[rows, :] + y
        ms = jnp.mean(x * x, axis=-1, keepdims=True)
        o_ref[rows, :] = x * lax.rsqrt(ms + NORM_EPS) * g_ref[...]

    @pl.when(i == last)
    def _():
        wait_slot(1 - slot)


def _combine(x1, g, y_rows, ids, wcol, padded_offsets):
    n, d = x1.shape
    _, row_tiles, lanes = y_rows.shape
    tm = min(COMBINE_TM, n)
    nt = n // tm
    idr = _tile_index_table(ids, tm)
    grid_spec = pltpu.PrefetchScalarGridSpec(
        num_scalar_prefetch=1,
        grid=(nt,),
        in_specs=[
            pl.BlockSpec((1, 1, 2 * TOP_K * tm), lambda i, off: (i, 0, 0), memory_space=pltpu.SMEM),
            pl.BlockSpec((1, 1, 2 * TOP_K * tm), lambda i, off: (jnp.minimum(i + 1, nt - 1), 0, 0),
                         memory_space=pltpu.SMEM),
            pl.BlockSpec((tm, d), lambda i, off: (i, 0)),
            pl.BlockSpec((1, d), lambda i, off: (0, 0)),
            pl.BlockSpec((tm, LANES), lambda i, off: (i, 0)),
            pl.BlockSpec(memory_space=pl.ANY),
        ],
        out_specs=pl.BlockSpec((tm, d), lambda i, off: (i, 0)),
        scratch_shapes=[pltpu.VMEM((2, TOP_K * tm, row_tiles, lanes), jnp.uint32),
                        pltpu.SemaphoreType.DMA((2,))],
    )
    return pl.pallas_call(
        _combine_kernel,
        grid_spec=grid_spec,
        out_shape=jax.ShapeDtypeStruct((n, d), f32),
        compiler_params=_cparams("arbitrary"),
        name="combine_norm",
    )(padded_offsets, idr, idr, x1, g.reshape(1, d), wcol, y_rows)


def _layer(x2, bsz, seq, norm_mix_g, w_in, b_gate, conv_w, conv_b, lru_wr, lru_br, lru_wi, lru_bi,
           lru_lambda, w_lru_out, pool_w, pool_scale, w_out, norm_ffn_g, router_wg, router_bg,
           router_we, router_be, exp_w_gu, exp_w_down):
    n, d = x2.shape
    d_lru = w_lru_out.shape[0]
    n_pool_groups, pool_gd, pool_od = pool_w.shape
    d_pool = n_pool_groups * pool_gd
    col_gate, col_pool, col_merge = d_lru, 2 * d_lru, 2 * d_lru + d_pool
    tn = min(MM_TN, pool_od, d_pool)
    tm = min(MM_TM, n)

    xn = _rmsnorm(x2, norm_mix_g, bf16)
    hg = _inproj_lru(xn, w_in, col_gate, conv_w, conv_b, lru_wr, lru_br, lru_wi, lru_bi, lru_lambda,
                     seq)
    v = _matmul(xn, w_in, col_pool, d_pool, f32, lambda acc: acc, tm, tn, name="in_proj_pool")
    mg = _matmul(xn, w_in, col_merge, 2 * d, bf16,
                 lambda acc, b: jax.nn.sigmoid(acc + b), tm, tn,
                 extras=(b_gate.reshape(1, 2 * d),),
                 extra_specs=(pl.BlockSpec((1, tn), lambda j, i: (0, j)),),
                 name="in_proj_merge")

    pooled = _pool_branch(v, bsz, seq)

    assert pool_od % tn == 0
    per_g = pool_od // tn

    def merge_epilogue(acc, pooled_t, pw, ps, mga, mgb):
        yb = jnp.dot(pooled_t, pw[0].astype(bf16), preferred_element_type=f32) * ps
        return mga.astype(f32) * acc + mgb.astype(f32) * yb

    merged = _matmul(
        hg, w_lru_out, 0, d, bf16, merge_epilogue, tm, tn,
        extras=(pooled, pool_w, pool_scale.reshape(1, d), mg, mg),
        extra_specs=(pl.BlockSpec((tm, pool_gd), lambda j, i: (i, j // per_g)),
                     pl.BlockSpec((1, pool_gd, tn), lambda j, i: (j // per_g, 0, j % per_g)),
                     pl.BlockSpec((1, tn), lambda j, i: (0, j)),
                     pl.BlockSpec((tm, tn), lambda j, i: (i, j)),
                     pl.BlockSpec((tm, tn), lambda j, i: (i, j + d // tn))),
        name="lru_out_merge")

    x1 = _matmul(merged, w_out, 0, d, f32, lambda acc, xr: xr + acc, tm, tn,
                 extras=(x2,), extra_specs=(pl.BlockSpec((tm, tn), lambda j, i: (i, j)),),
                 name="out_proj")

    h2, ids, wcol, counts_f = _router(x1, norm_ffn_g, router_wg, router_bg, router_we, router_be)
    n_experts = exp_w_gu.shape[0]
    max_rows = n * TOP_K + n_experts * EXPERT_BLOCK
    offsets, pad_start, pad_count, schedule, n_used = _routing_tables(
        counts_f[:, 0].astype(jnp.int32), EXPERT_BLOCK, max_rows // EXPERT_BLOCK)
    x_sorted = _dispatch(h2, ids, offsets, pad_start, pad_count, max_rows, EXPERT_BLOCK)
    y_rows = _experts(x_sorted, exp_w_gu, exp_w_down, schedule, n_used, EXPERT_BLOCK)
    return x1, y_rows, ids, wcol, offsets


def kernel(x, norm_mix_g, w_in, b_gate, conv_w, conv_b, lru_wr, lru_br, lru_wi, lru_bi, lru_lambda,
           w_lru_out, pool_w, pool_scale, w_out, norm_ffn_g, router_wg, router_bg, router_we,
           router_be, exp_w_gu, exp_w_down, norm_final_g):
    bsz, seq, d = x.shape
    depth = w_in.shape[0]
    assert depth == 1
    x2 = x.reshape(bsz * seq, d)
    x1, y_rows, ids, wcol, offsets = _layer(
        x2, bsz, seq, norm_mix_g[0], w_in[0], b_gate[0], conv_w[0], conv_b[0], lru_wr[0], lru_br[0],
        lru_wi[0], lru_bi[0], lru_lambda[0], w_lru_out[0], pool_w[0], pool_scale[0], w_out[0],
        norm_ffn_g[0], router_wg[0], router_bg[0], router_we[0], router_be[0], exp_w_gu[0],
        exp_w_down[0])
    out = _combine(x1, norm_final_g, y_rows, ids, wcol, offsets)
    return out.reshape(bsz, seq, d)
```

```python
import functools

import jax
import jax.numpy as jnp
from jax import lax
from jax.experimental import pallas as pl
from jax.experimental.pallas import tpu as pltpu

NORM_EPS = 1e-6
LRU_C = 8.0
POOL_WINDOWS = (2, 4, 8, 16)
TOP_K = 2

VMEM_LIMIT_BYTES = 56 * 1024 * 1024
SUBLANES = 8
LANES = 128

MM_TM = 512
MM_TN = 1024
NORM_TM = 512
LRU_TM = 1024
LRU_PARTS = 4
POOL_T = 512
ROUTER_TM = 256
EXPERT_BLOCK = 256
DISPATCH_TM = 1024
EXPERTS_VMEM_LIMIT_BYTES = 63 * 1024 * 1024
COMBINE_TM = 256
COMBINE_PIECES = 8

f32 = jnp.float32
bf16 = jnp.bfloat16


def _cparams(*sem):
    return pltpu.CompilerParams(dimension_semantics=sem, vmem_limit_bytes=VMEM_LIMIT_BYTES)


def _rmsnorm_kernel(x_ref, g_ref, o_ref):
    x = x_ref[...]
    ms = jnp.mean(x * x, axis=-1, keepdims=True)
    o_ref[...] = (x * lax.rsqrt(ms + NORM_EPS) * g_ref[...]).astype(o_ref.dtype)


def _rmsnorm(x, g, out_dtype):
    n, d = x.shape
    return pl.pallas_call(
        _rmsnorm_kernel,
        grid=(n // NORM_TM,),
        in_specs=[pl.BlockSpec((NORM_TM, d), lambda i: (i, 0)),
                  pl.BlockSpec((1, d), lambda i: (0, 0))],
        out_specs=pl.BlockSpec((NORM_TM, d), lambda i: (i, 0)),
        out_shape=jax.ShapeDtypeStruct((n, d), out_dtype),
        compiler_params=_cparams("arbitrary"),
        name="rmsnorm",
    )(x, g.reshape(1, d))


def _mm_kernel(*refs, n_extra, epilogue):
    lhs_ref, w_ref = refs[0], refs[1]
    extras = refs[2:2 + n_extra]
    o_ref = refs[2 + n_extra]
    wbf_ref = refs[3 + n_extra]

    @pl.when(pl.program_id(1) == 0)
    def _():
        wbf_ref[...] = w_ref[...].astype(bf16)

    acc = jnp.dot(lhs_ref[...], wbf_ref[...], preferred_element_type=f32)
    o_ref[...] = epilogue(acc, *[e[...] for e in extras]).astype(o_ref.dtype)


def _matmul(lhs, w, col_off, ncols, out_dtype, epilogue, tm, tn, extras=(), extra_specs=(),
            name="mm"):
    m, k = lhs.shape
    assert m % tm == 0 and ncols % tn == 0 and col_off % tn == 0
    off = col_off // tn
    return pl.pallas_call(
        functools.partial(_mm_kernel, n_extra=len(extras), epilogue=epilogue),
        grid=(ncols // tn, m // tm),
        in_specs=[pl.BlockSpec((tm, k), lambda j, i: (i, 0)),
                  pl.BlockSpec((k, tn), lambda j, i: (0, j + off), pipeline_mode=pl.Buffered(1))]
        + list(extra_specs),
        out_specs=pl.BlockSpec((tm, tn), lambda j, i: (i, j)),
        out_shape=jax.ShapeDtypeStruct((m, ncols), out_dtype),
        scratch_shapes=[pltpu.VMEM((k, tn), bf16)],
        compiler_params=_cparams("arbitrary", "arbitrary"),
        name=name,
    )(lhs, w, *extras)


def _lru_part(acc, tails, carry, cw, cb, wr, br, wi, bi, c_sp, u_scr, g_scr, o_scr):
    rows, c = acc.shape[0], acc.shape[1] // 2
    groups, kw, n_slabs = rows // SUBLANES, cw.shape[0], c // LANES
    grow = lax.broadcasted_iota(jnp.int32, (groups, LANES), 0)
    sub3 = lax.broadcasted_iota(jnp.int32, (groups // SUBLANES, SUBLANES, LANES), 1)
    sub8 = lax.broadcasted_iota(jnp.int32, (SUBLANES, LANES), 0)

    def piece(scr, s, j):
        return scr[s, pl.ds(j, groups, stride=SUBLANES), :]

    uc_pieces, new_tails = [], {j: [] for j in tails}
    for s in range(n_slabs):
        lanes = slice(s * LANES, (s + 1) * LANES)
        u_scr[s] = acc[:, lanes]
        g_scr[s] = acc[:, c + s * LANES:c + (s + 1) * LANES]
        u = [piece(u_scr, s, j) for j in range(SUBLANES)]
        before = {}
        for j in tails:
            before[j] = jnp.where(grow == 0, tails[j][:, lanes], pltpu.roll(u[j], 1, 0))
            new_tails[j].append(u[j][groups - 1:groups])
        ucs = []
        for j in range(SUBLANES):
            v = u[j] * cw[kw - 1:kw, lanes] + cb[:, lanes]
            for k in range(1, kw):
                src = u[j - k] if j >= k else before[j - k + SUBLANES]
                v = v + src * cw[kw - 1 - k:kw - k, lanes]
            ucs.append(v)
        uc_pieces.append(jnp.concatenate(ucs, axis=0))
    uc = jnp.concatenate(uc_pieces, axis=1)

    ub = uc.astype(bf16)
    r_logit = jnp.dot(ub, wr, preferred_element_type=f32) + br
    i_logit = jnp.dot(ub, wi, preferred_element_type=f32) + bi
    neg_log_a = jax.nn.sigmoid(r_logit) * c_sp
    a = jnp.exp(-neg_log_a)
    z = jnp.tanh(neg_log_a) * (1.0 + a * a)
    beta = jnp.where(z > 0.0, z * lax.rsqrt(z), 0.0)
    b = beta * (jax.nn.sigmoid(i_logit) * uc)

    carries = []
    for s in range(n_slabs):
        lanes = slice(s * LANES, (s + 1) * LANES)
        ps, hs = [], []
        for j in range(SUBLANES):
            aj = a[j * groups:(j + 1) * groups, lanes]
            bj = b[j * groups:(j + 1) * groups, lanes]
            hs.append(bj if j == 0 else aj * hs[-1] + bj)
            ps.append(aj if j == 0 else aj * ps[-1])
        pt = ps[-1].reshape(groups // SUBLANES, SUBLANES, LANES)
        ht = hs[-1].reshape(groups // SUBLANES, SUBLANES, LANES)
        d = 1
        while d < SUBLANES:
            keep = sub3 >= d
            ht = jnp.where(keep, pt * pltpu.roll(ht, d, 1) + ht, ht)
            pt = jnp.where(keep, pt * pltpu.roll(pt, d, 1), pt)
            d *= 2
        state = carry[:, lanes]
        incoming = []
        for q in range(groups // SUBLANES):
            after = ht[q] + pt[q] * state
            incoming.append(jnp.where(sub8 == 0, state, pltpu.roll(after, 1, 0)))
            state = after[SUBLANES - 1:SUBLANES]
        carries.append(state)
        incoming = jnp.concatenate(incoming, axis=0)
        for j in range(SUBLANES):
            h = hs[j] + ps[j] * incoming
            o_scr[s, pl.ds(j, groups, stride=SUBLANES), :] = h * jax.nn.gelu(piece(g_scr, s, j))
    out = jnp.concatenate([o_scr[s] for s in range(n_slabs)], axis=1)
    new_tails = {j: jnp.concatenate(v, axis=1) for j, v in new_tails.items()}
    return out, new_tails, jnp.concatenate(carries, axis=1)


def _inproj_lru_kernel(xn_ref, wu_ref, wg_ref, cw_ref, cb_ref, wr_ref, br_ref, wi_ref, bi_ref,
                       lam_ref, o_ref, wbf_ref, tail_ref, carry_ref, u_scr, g_scr, o_scr, *,
                       tiles_per_seq, parts):
    hd = wu_ref.shape[1]
    i = pl.program_id(1)

    @pl.when(i == 0)
    def _():
        wbf_ref[:, :hd] = wu_ref[...].astype(bf16)
        wbf_ref[:, hd:] = wg_ref[...].astype(bf16)

    @pl.when(i % tiles_per_seq == 0)
    def _():
        tail_ref[...] = jnp.zeros_like(tail_ref)
        carry_ref[...] = jnp.zeros_like(carry_ref)

    cw, cb = cw_ref[...], cb_ref[...]
    wr, wi = wr_ref[0].astype(bf16), wi_ref[0].astype(bf16)
    br, bi = br_ref[...], bi_ref[...]
    c_sp = LRU_C * jax.nn.softplus(-lam_ref[...])
    kw = cw.shape[0]
    tails = {j: tail_ref[j:j + 1, :] for j in range(SUBLANES - kw + 1, SUBLANES)}
    carry = carry_ref[...]
    rows = xn_ref.shape[0] // parts
    accs = [jnp.dot(xn_ref[p * rows:(p + 1) * rows, :], wbf_ref[...], preferred_element_type=f32)
            for p in range(parts)]
    for p in range(parts):
        hg, tails, carry = _lru_part(accs[p], tails, carry, cw, cb, wr, br, wi, bi, c_sp,
                                     u_scr, g_scr, o_scr)
        o_ref[p * rows:(p + 1) * rows, :] = hg.astype(o_ref.dtype)
    for j, row in tails.items():
        tail_ref[j:j + 1, :] = row
    carry_ref[...] = carry


def _inproj_lru(xn, w_in, col_gate, conv_w, conv_b, wr, br, wi, bi, lam, seq):
    n, k = xn.shape
    heads, hd, _ = wr.shape
    d_lru = heads * hd
    kw = conv_w.shape[0]
    tm = min(LRU_TM, seq)
    assert seq % tm == 0 and col_gate % hd == 0 and (tm // LRU_PARTS) % SUBLANES == 0
    gate_off = col_gate // hd
    vec = lambda h, i: (0, h)
    return pl.pallas_call(
        functools.partial(_inproj_lru_kernel, tiles_per_seq=seq // tm, parts=LRU_PARTS),
        grid=(heads, n // tm),
        in_specs=[pl.BlockSpec((tm, k), lambda h, i: (i, 0)),
                  pl.BlockSpec((k, hd), lambda h, i: (0, h)),
                  pl.BlockSpec((k, hd), lambda h, i: (0, gate_off + h)),
                  pl.BlockSpec((kw, hd), vec),
                  pl.BlockSpec((1, hd), vec),
                  pl.BlockSpec((1, hd, hd), lambda h, i: (h, 0, 0)),
                  pl.BlockSpec((1, hd), vec),
                  pl.BlockSpec((1, hd, hd), lambda h, i: (h, 0, 0)),
                  pl.BlockSpec((1, hd), vec),
                  pl.BlockSpec((1, hd), vec)],
        out_specs=pl.BlockSpec((tm, hd), lambda h, i: (i, h)),
        out_shape=jax.ShapeDtypeStruct((n, d_lru), bf16),
        scratch_shapes=[pltpu.VMEM((k, 2 * hd), bf16), pltpu.VMEM((SUBLANES, hd), f32),
                        pltpu.VMEM((1, hd), f32)]
        + [pltpu.VMEM((hd // LANES, tm // LRU_PARTS, LANES), f32)] * 3,
        compiler_params=_cparams("arbitrary", "arbitrary"),
        name="in_proj_rg_lru",
    )(xn, w_in, w_in, conv_w, conv_b.reshape(1, d_lru), wr, br.reshape(1, d_lru), wi,
      bi.reshape(1, d_lru), lam.reshape(1, d_lru))


_POOL_HALO = 16


def _pool_kernel(v_ref, o_ref, tail_ref):
    t_len, c = v_ref.shape
    gd = c // len(POOL_WINDOWS)
    t = pl.program_id(1)

    @pl.when(t == 0)
    def _():
        tail_ref[...] = jnp.zeros_like(tail_ref)

    v = v_ref[...]
    ext = jnp.concatenate([tail_ref[...], v], axis=0)
    tail_ref[...] = v[t_len - _POOL_HALO:]
    pos1 = t * t_len + lax.broadcasted_iota(jnp.int32, (t_len, 1), 0) + 1
    for g, w in enumerate(POOL_WINDOWS):
        s = ext[:, g * gd:(g + 1) * gd]
        span = 1
        while span < w:
            s = s + pltpu.roll(s, span, 0)
            span *= 2
        count = jnp.minimum(pos1, w).astype(f32)
        o_ref[:, g * gd:(g + 1) * gd] = (
            s[_POOL_HALO:] / count - v[:, g * gd:(g + 1) * gd]).astype(o_ref.dtype)


def _pool_branch(v, bsz, seq):
    n, c = v.shape
    assert max(POOL_WINDOWS) <= _POOL_HALO
    t_len = min(POOL_T, seq)
    nt = seq // t_len
    return pl.pallas_call(
        _pool_kernel,
        grid=(bsz, nt),
        in_specs=[pl.BlockSpec((t_len, c), lambda b, t: (b * nt + t, 0))],
        out_specs=pl.BlockSpec((t_len, c), lambda b, t: (b * nt + t, 0)),
        out_shape=jax.ShapeDtypeStruct((n, c), bf16),
        scratch_shapes=[pltpu.VMEM((_POOL_HALO, c), f32)],
        compiler_params=_cparams("arbitrary", "arbitrary"),
        name="pool",
    )(v)


_HI16 = 0xFFFF0000


def _pack_bf16_pairs(xb):
    half = xb.shape[1] // 2
    bits = lax.bitcast_convert_type(xb.astype(f32), jnp.uint32)
    return (bits[:, :half] >> 16) | (bits[:, half:] & jnp.uint32(_HI16))


def _unpack_bf16_pairs(p):
    lo = lax.bitcast_convert_type(p << 16, f32)
    hi = lax.bitcast_convert_type(p & jnp.uint32(_HI16), f32)
    return jnp.concatenate([lo, hi], axis=1)


def _rows_to_tiles(p):
    return p.reshape(p.shape[0], p.shape[1] // LANES, LANES)


def _tiles_to_rows(t):
    return t.reshape(t.shape[0], t.shape[1] * t.shape[2])


def _router_kernel(x_ref, g_ref, wc_ref, bc_ref, h_ref, ids_ref, wcol_ref, counts_ref, base_ref,
                   w2_ref, *,
                   n_groups, per_group):
    x = x_ref[...]
    ms = jnp.mean(x * x, axis=-1, keepdims=True)
    h = x * lax.rsqrt(ms + NORM_EPS) * g_ref[...]

    hh = h.astype(bf16)
    h_ref[...] = _rows_to_tiles(_pack_bf16_pairs(hh))
    hl = (h - hh.astype(f32)).astype(bf16)

    @pl.when(pl.program_id(0) == 0)
    def _():
        wc = wc_ref[...]
        wh = wc.astype(bf16)
        w2_ref[:, :LANES] = wh
        w2_ref[:, LANES:] = (wc - wh.astype(f32)).astype(bf16)

    hi_part = jnp.dot(hh, w2_ref[...], preferred_element_type=f32)
    logits = (hi_part[:, :LANES] + hi_part[:, LANES:]
              + jnp.dot(hl, w2_ref[:, :LANES], preferred_element_type=f32))
    lt = logits.T + bc_ref[...]

    tm = x.shape[0]
    row = lax.broadcasted_iota(jnp.int32, (per_group, tm), 0)

    def softmax_rows(z):
        e = jnp.exp(z - jnp.max(z, axis=0, keepdims=True))
        return e / jnp.sum(e, axis=0, keepdims=True)

    def top1(p):
        pmax = jnp.max(p, axis=0, keepdims=True)
        idx = jnp.min(jnp.where(p == pmax, row, per_group), axis=0, keepdims=True)
        return pmax, idx

    assert n_groups == per_group == SUBLANES
    g_p, g_idx = top1(softmax_rows(lt[0:n_groups]))
    sel = jnp.zeros((per_group, tm), f32)
    for g in range(n_groups):
        lo = n_groups + g * per_group
        sel = jnp.where(g_idx == g, lt[lo:lo + per_group], sel)
    p = softmax_rows(sel)
    p1, i1 = top1(p)
    p2, i2 = top1(jnp.where(row == i1, -1.0, p))
    den = p1 + p2
    w1 = g_p * p1 / den
    w2 = g_p * p2 / den
    id1 = g_idx * per_group + i1
    id2 = g_idx * per_group + i2
    lane_row = lax.broadcasted_iota(jnp.int32, (LANES, tm), 0)
    wcol_ref[...] = jnp.where(lane_row == 0, w1, jnp.where(lane_row == 1, w2, 0.0)).T

    @pl.when(pl.program_id(0) == 0)
    def _():
        base_ref[...] = jnp.zeros_like(base_ref)

    n_exp = n_groups * per_group
    e_iota = lax.broadcasted_iota(jnp.int32, (n_exp, tm), 0)
    hot1 = e_iota == id1
    hot2 = e_iota == id2
    hot = jnp.concatenate([hot1, hot2], axis=0).astype(f32).astype(bf16)
    src = lax.broadcasted_iota(jnp.int32, (tm, 2 * tm), 0)
    dst = lax.broadcasted_iota(jnp.int32, (tm, 2 * tm), 1)
    tri_ones = ((src < dst) | (dst >= tm)).astype(f32).astype(bf16)
    cnt = jnp.dot(hot, tri_ones, preferred_element_type=f32)
    base = base_ref[...]
    before1 = cnt[:n_exp, :tm] + base
    tot1 = cnt[:n_exp, tm:]
    before2 = cnt[n_exp:, :tm] + base + tot1
    rank1 = jnp.sum(jnp.where(hot1, before1, 0.0), axis=0, keepdims=True).astype(jnp.int32)
    rank2 = jnp.sum(jnp.where(hot2, before2, 0.0), axis=0, keepdims=True).astype(jnp.int32)
    base = base + tot1 + cnt[n_exp:, tm:]
    base_ref[...] = base
    counts_ref[...] = base
    ids_ref[...] = jnp.where(row == 0, id1, jnp.where(row == 1, id2,
                             jnp.where(row == 2, rank1, jnp.where(row == 3, rank2, 0))))


def _router(x1, g, wg, bg, we, be):
    n, d = x1.shape
    n_groups = wg.shape[1]
    per_group = we.shape[1] // n_groups
    n_exp = we.shape[1]
    ncat = n_groups + n_exp
    assert ncat <= LANES and n_exp % SUBLANES == 0
    wc = jnp.pad(jnp.concatenate([wg, we], axis=1), ((0, 0), (0, LANES - ncat)))
    bc = jnp.pad(jnp.concatenate([bg, be]), (0, LANES - ncat)).reshape(LANES, 1)
    tm = min(ROUTER_TM, n)
    return pl.pallas_call(
        functools.partial(_router_kernel, n_groups=n_groups, per_group=per_group),
        grid=(n // tm,),
        in_specs=[pl.BlockSpec((tm, d), lambda i: (i, 0)),
                  pl.BlockSpec((1, d), lambda i: (0, 0)),
                  pl.BlockSpec((d, LANES), lambda i: (0, 0)),
                  pl.BlockSpec((LANES, 1), lambda i: (0, 0))],
        out_specs=[pl.BlockSpec((tm, d // 2 // LANES, LANES), lambda i: (i, 0, 0)),
                   pl.BlockSpec((SUBLANES, tm), lambda i: (0, i)),
                   pl.BlockSpec((tm, LANES), lambda i: (i, 0)),
                   pl.BlockSpec((n_exp, tm), lambda i: (0, 0))],
        out_shape=[jax.ShapeDtypeStruct((n, d // 2 // LANES, LANES), jnp.uint32),
                   jax.ShapeDtypeStruct((SUBLANES, n), jnp.int32),
                   jax.ShapeDtypeStruct((n, LANES), f32),
                   jax.ShapeDtypeStruct((n_exp, tm), f32)],
        scratch_shapes=[pltpu.VMEM((n_exp, tm), f32), pltpu.VMEM((d, 2 * LANES), bf16)],
        compiler_params=_cparams("arbitrary"),
        name="norm_router",
    )(x1, g.reshape(1, d), wc, bc)


def _routing_tables(counts, blk, n_blocks):
    n_experts = counts.shape[0]
    padded = (counts + blk - 1) // blk * blk
    e = jnp.arange(n_experts, dtype=jnp.int32)
    padded_ends = jnp.sum(jnp.where(e[None, :] <= e[:, None], padded[None, :], 0), axis=1)
    padded_offsets = padded_ends - padded
    block_start = jnp.arange(n_blocks, dtype=jnp.int32) * blk
    block_expert = jnp.minimum(
        jnp.sum((padded_ends[None, :] <= block_start[:, None]).astype(jnp.int32), axis=1),
        n_experts - 1)
    n_used = (padded_ends[n_experts - 1:] // blk).astype(jnp.int32)
    b = jnp.arange(n_blocks, dtype=jnp.int32)
    used = b < n_used[0]
    prev_expert = jnp.concatenate([jnp.full((1,), -1, jnp.int32), block_expert[:-1]])
    first = used & (block_expert != prev_expert)
    slot = (jnp.cumsum(first.astype(jnp.int32)) - 1) % 2
    first_pos = jnp.where(first, b, n_blocks)
    next_first = lax.cummin(jnp.concatenate([first_pos[1:], jnp.full((1,), n_blocks, jnp.int32)]),
                            axis=0, reverse=True)
    next_expert = jnp.where(next_first < n_blocks,
                            block_expert[jnp.minimum(next_first, n_blocks - 1)], -1)
    schedule = jnp.stack([block_expert, slot, first.astype(jnp.int32), next_expert]).astype(jnp.int32)
    return padded_offsets, padded_offsets + counts, padded - counts, schedule, n_used


def _tile_index_table(ids, tm):
    n = ids.shape[1]
    rows = 2 * TOP_K
    return ids[:rows].reshape(rows, n // tm, tm).transpose(1, 0, 2).reshape(n // tm, 1, rows * tm)


_DMA_UNROLL = 8


def _dispatch_kernel(off_ref, pad_start_ref, pad_count_ref, idr_ref, h_ref, xs_hbm, zero_ref, sem,
                     zsem):
    tm = h_ref.shape[0]

    def issue(g, carry):
        for u in range(_DMA_UNROLL):
            r = g * _DMA_UNROLL + u
            for k in range(TOP_K):
                dst = off_ref[idr_ref[0, 0, k * tm + r]] + idr_ref[0, 0, (TOP_K + k) * tm + r]
                pltpu.make_async_copy(h_ref.at[pl.ds(r, 1)], xs_hbm.at[pl.ds(dst, 1)],
                                      sem.at[0]).start(priority=k % 2)
        return carry
    lax.fori_loop(0, tm // _DMA_UNROLL, issue, 0)

    @pl.when(pl.program_id(0) == 0)
    def _():
        zero_ref[...] = jnp.zeros_like(zero_ref)

        chunk = zero_ref.shape[0]
        chunks = [chunk >> s for s in range(chunk.bit_length())]

        def pad_copies(e):
            count = pad_count_ref[e]
            for c in chunks:
                offset = count & ~(2 * c - 1)
                yield (count & c) != 0, pltpu.make_async_copy(
                    zero_ref.at[pl.ds(0, c)], xs_hbm.at[pl.ds(pad_start_ref[e] + offset, c)],
                    zsem.at[0])

        def start_expert(e, carry):
            for present, copy in pad_copies(e):
                pl.when(present)(copy.start)
            return carry

        def wait_expert(e, carry):
            for present, copy in pad_copies(e):
                pl.when(present)(copy.wait)
            return carry

        n_experts = pad_start_ref.shape[0]
        lax.fori_loop(0, n_experts, start_expert, 0)
        lax.fori_loop(0, n_experts, wait_expert, 0)

    for _ in range(TOP_K):
        pltpu.make_async_copy(h_ref, xs_hbm.at[pl.ds(0, tm)], sem.at[0]).wait()


def _dispatch(h_packed, ids, padded_offsets, pad_start, pad_count, max_rows, blk):
    n, row_tiles, lanes = h_packed.shape
    tm = min(DISPATCH_TM, n)
    idr = _tile_index_table(ids, tm)
    grid_spec = pltpu.PrefetchScalarGridSpec(
        num_scalar_prefetch=3,
        grid=(n // tm,),
        in_specs=[pl.BlockSpec((1, 1, 2 * TOP_K * tm), lambda i, *_: (i, 0, 0),
                               memory_space=pltpu.SMEM),
                  pl.BlockSpec((tm, row_tiles, lanes), lambda i, *_: (i, 0, 0))],
        out_specs=pl.BlockSpec(memory_space=pl.ANY),
        scratch_shapes=[pltpu.VMEM((pl.next_power_of_2(blk) // 2, row_tiles, lanes), jnp.uint32),
                        pltpu.SemaphoreType.DMA((1,)), pltpu.SemaphoreType.DMA((1,))],
    )
    return pl.pallas_call(
        _dispatch_kernel,
        grid_spec=grid_spec,
        out_shape=jax.ShapeDtypeStruct((max_rows, row_tiles, lanes), jnp.uint32),
        compiler_params=_cparams("arbitrary"),
        name="dispatch",
    )(padded_offsets, pad_start, pad_count, idr, h_packed)


def _moe_kernel(bexp_ref, slot_ref, first_ref, next_ref, nused_ref, x_ref, wgu_hbm, wdn_hbm, o_ref,
                wgu_buf, wdn_buf, sem):
    b = pl.program_id(0)
    n_used = nused_ref[0]

    def weight_copies(e, s):
        return (pltpu.make_async_copy(wgu_hbm.at[e], wgu_buf.at[s], sem.at[s, 0]),
                pltpu.make_async_copy(wdn_hbm.at[e], wdn_buf.at[s], sem.at[s, 1]))

    @pl.when(b == 0)
    def _():
        for c in weight_copies(bexp_ref[0], 0):
            c.start()

    @pl.when(first_ref[b] == 1)
    def _():
        for c in weight_copies(bexp_ref[b], slot_ref[b]):
            c.wait()

        @pl.when(next_ref[b] >= 0)
        def _():
            for c in weight_copies(next_ref[b], 1 - slot_ref[b]):
                c.start()

    @pl.when(b < n_used)
    def _():
        s = slot_ref[b]
        x = _unpack_bf16_pairs(_tiles_to_rows(x_ref[...])).astype(bf16)
        gu = jnp.dot(x, wgu_buf[s].astype(bf16), preferred_element_type=f32)
        f = gu.shape[1] // 2
        act = (jax.nn.silu(gu[:, :f]) * gu[:, f:]).astype(bf16)
        y = jnp.dot(act, wdn_buf[s].astype(bf16), preferred_element_type=f32)
        o_ref[...] = _rows_to_tiles(_pack_bf16_pairs(y.astype(bf16)))

    @pl.when(b >= n_used)
    def _():
        o_ref[...] = jnp.zeros_like(o_ref)


def _experts(x_sorted, w_gu, w_down, schedule, n_used, blk):
    max_rows, row_tiles, lanes = x_sorted.shape
    _, d, f2 = w_gu.shape
    assert d == 2 * row_tiles * lanes
    nb = max_rows // blk
    row_block = (blk, row_tiles, lanes)
    grid_spec = pltpu.PrefetchScalarGridSpec(
        num_scalar_prefetch=5,
        grid=(nb,),
        in_specs=[
            pl.BlockSpec(row_block, lambda b, be, sl, fi, nx, nu: (jnp.minimum(b, nu[0] - 1), 0, 0)),
            pl.BlockSpec(memory_space=pl.ANY),
            pl.BlockSpec(memory_space=pl.ANY),
        ],
        out_specs=pl.BlockSpec(row_block, lambda b, be, sl, fi, nx, nu: (jnp.minimum(b, nu[0]), 0, 0)),
        scratch_shapes=[pltpu.VMEM((2, d, f2), f32), pltpu.VMEM((2, f2 // 2, d), f32),
                        pltpu.SemaphoreType.DMA((2, 2))],
    )
    return pl.pallas_call(
        _moe_kernel,
        grid_spec=grid_spec,
        out_shape=jax.ShapeDtypeStruct(x_sorted.shape, jnp.uint32),
        compiler_params=pltpu.CompilerParams(dimension_semantics=("arbitrary",),
                                             vmem_limit_bytes=EXPERTS_VMEM_LIMIT_BYTES),
        name="experts",
    )(schedule[0], schedule[1], schedule[2], schedule[3], n_used, x_sorted, w_gu, w_down)


def _start_expert_row_copy(off_ref, idr_ref, tm, r, y_hbm, buf, sem, slot):
    for k in range(TOP_K):
        src = off_ref[idr_ref[0, 0, k * tm + r]] + idr_ref[0, 0, (TOP_K + k) * tm + r]
        pltpu.make_async_copy(y_hbm.at[pl.ds(src, 1)], buf.at[slot, pl.ds(k * tm + r, 1)],
                              sem.at[slot]).start(priority=k % 2)


def _combine_kernel(off_ref, idr_cur_ref, idr_nxt_ref, x_ref, g_ref, w_ref, y_hbm, o_ref, buf, sem):
    i = pl.program_id(0)
    last = pl.num_programs(0) - 1
    tm = x_ref.shape[0]
    slot = i % 2

    def wait_slot(s):
        pltpu.make_async_copy(y_hbm.at[pl.ds(0, buf.shape[1])], buf.at[s], sem.at[s]).wait()

    @pl.when(i == 0)
    def _():
        def issue(g, carry):
            for u in range(_DMA_UNROLL):
                _start_expert_row_copy(off_ref, idr_cur_ref, tm, g * _DMA_UNROLL + u, y_hbm, buf, sem, 0)
            return carry
        lax.fori_loop(0, tm // _DMA_UNROLL, issue, 0)

    wait_slot(slot)
    rc = tm // COMBINE_PIECES
    for c in range(COMBINE_PIECES):
        for r in range(c * rc, (c + 1) * rc):
            _start_expert_row_copy(off_ref, idr_nxt_ref, tm, r, y_hbm, buf, sem, 1 - slot)
        rows = slice(c * rc, (c + 1) * rc)
        w = w_ref[rows, :]
        y = (w[:, 0:1] * _unpack_bf16_pairs(_tiles_to_rows(buf[slot, c * rc:(c + 1) * rc]))
             + w[:, 1:2] * _unpack_bf16_pairs(_tiles_to_rows(buf[slot, tm + c * rc:tm + (c + 1) * rc])))
        x = x_ref[rows, :] + y
        ms = jnp.mean(x * x, axis=-1, keepdims=True)
        o_ref[rows, :] = x * lax.rsqrt(ms + NORM_EPS) * g_ref[...]

    @pl.when(i == last)
    def _():
        wait_slot(1 - slot)


def _combine(x1, g, y_rows, ids, wcol, padded_offsets):
    n, d = x1.shape
    _, row_tiles, lanes = y_rows.shape
    tm = min(COMBINE_TM, n)
    nt = n // tm
    idr = _tile_index_table(ids, tm)
    grid_spec = pltpu.PrefetchScalarGridSpec(
        num_scalar_prefetch=1,
        grid=(nt,),
        in_specs=[
            pl.BlockSpec((1, 1, 2 * TOP_K * tm), lambda i, off: (i, 0, 0), memory_space=pltpu.SMEM),
            pl.BlockSpec((1, 1, 2 * TOP_K * tm), lambda i, off: (jnp.minimum(i + 1, nt - 1), 0, 0),
                         memory_space=pltpu.SMEM),
            pl.BlockSpec((tm, d), lambda i, off: (i, 0)),
            pl.BlockSpec((1, d), lambda i, off: (0, 0)),
            pl.BlockSpec((tm, LANES), lambda i, off: (i, 0)),
            pl.BlockSpec(memory_space=pl.ANY),
        ],
        out_specs=pl.BlockSpec((tm, d), lambda i, off: (i, 0)),
        scratch_shapes=[pltpu.VMEM((2, TOP_K * tm, row_tiles, lanes), jnp.uint32),
                        pltpu.SemaphoreType.DMA((2,))],
    )
    return pl.pallas_call(
        _combine_kernel,
        grid_spec=grid_spec,
        out_shape=jax.ShapeDtypeStruct((n, d), f32),
        compiler_params=_cparams("arbitrary"),
        name="combine_norm",
    )(padded_offsets, idr, idr, x1, g.reshape(1, d), wcol, y_rows)


def _layer(x2, bsz, seq, norm_mix_g, w_in, b_gate, conv_w, conv_b, lru_wr, lru_br, lru_wi, lru_bi,
           lru_lambda, w_lru_out, pool_w, pool_scale, w_out, norm_ffn_g, router_wg, router_bg,
           router_we, router_be, exp_w_gu, exp_w_down):
    n, d = x2.shape
    d_lru = w_lru_out.shape[0]
    n_pool_groups, pool_gd, pool_od = pool_w.shape
    d_pool = n_pool_groups * pool_gd
    col_gate, col_pool, col_merge = d_lru, 2 * d_lru, 2 * d_lru + d_pool
    tn = min(MM_TN, pool_od, d_pool)
    tm = min(MM_TM, n)

    xn = _rmsnorm(x2, norm_mix_g, bf16)
    hg = _inproj_lru(xn, w_in, col_gate, conv_w, conv_b, lru_wr, lru_br, lru_wi, lru_bi, lru_lambda,
                     seq)
    v = _matmul(xn, w_in, col_pool, d_pool, f32, lambda acc: acc, tm, tn, name="in_proj_pool")
    mg = _matmul(xn, w_in, col_merge, 2 * d, bf16,
                 lambda acc, b: jax.nn.sigmoid(acc + b), tm, tn,
                 extras=(b_gate.reshape(1, 2 * d),),
                 extra_specs=(pl.BlockSpec((1, tn), lambda j, i: (0, j)),),
                 name="in_proj_merge")

    pooled = _pool_branch(v, bsz, seq)

    assert pool_od % tn == 0
    per_g = pool_od // tn

    def merge_epilogue(acc, pooled_t, pw, ps, mga, mgb):
        yb = jnp.dot(pooled_t, pw[0].astype(bf16), preferred_element_type=f32) * ps
        return mga.astype(f32) * acc + mgb.astype(f32) * yb

    merged = _matmul(
        hg, w_lru_out, 0, d, bf16, merge_epilogue, tm, tn,
        extras=(pooled, pool_w, pool_scale.reshape(1, d), mg, mg),
        extra_specs=(pl.BlockSpec((tm, pool_gd), lambda j, i: (i, j // per_g)),
                     pl.BlockSpec((1, pool_gd, tn), lambda j, i: (j // per_g, 0, j % per_g)),
                     pl.BlockSpec((1, tn), lambda j, i: (0, j)),
                     pl.BlockSpec((tm, tn), lambda j, i: (i, j)),
                     pl.BlockSpec((tm, tn), lambda j, i: (i, j + d // tn))),
        name="lru_out_merge")

    x1 = _matmul(merged, w_out, 0, d, f32, lambda acc, xr: xr + acc, tm, tn,
                 extras=(x2,), extra_specs=(pl.BlockSpec((tm, tn), lambda j, i: (i, j)),),
                 name="out_proj")

    h2, ids, wcol, counts_f = _router(x1, norm_ffn_g, router_wg, router_bg, router_we, router_be)
    n_experts = exp_w_gu.shape[0]
    max_rows = n * TOP_K + n_experts * EXPERT_BLOCK
    offsets, pad_start, pad_count, schedule, n_used = _routing_tables(
        counts_f[:, 0].astype(jnp.int32), EXPERT_BLOCK, max_rows // EXPERT_BLOCK)
    x_sorted = _dispatch(h2, ids, offsets, pad_start, pad_count, max_rows, EXPERT_BLOCK)
    y_rows = _experts(x_sorted, exp_w_gu, exp_w_down, schedule, n_used, EXPERT_BLOCK)
    return x1, y_rows, ids, wcol, offsets


def kernel(x, norm_mix_g, w_in, b_gate, conv_w, conv_b, lru_wr, lru_br, lru_wi, lru_bi, lru_lambda,
           w_lru_out, pool_w, pool_scale, w_out, norm_ffn_g, router_wg, router_bg, router_we,
           router_be, exp_w_gu, exp_w_down, norm_final_g):
    bsz, seq, d = x.shape
    depth = w_in.shape[0]
    assert depth == 1
    x2 = x.reshape(bsz * seq, d)
    x1, y_rows, ids, wcol, offsets = _layer(
        x2, bsz, seq, norm_mix_g[0], w_in[0], b_gate[0], conv_w[0], conv_b[0], lru_wr[0], lru_br[0],
        lru_wi[0], lru_bi[0], lru_lambda[0], w_lru_out[0], pool_w[0], pool_scale[0], w_out[0],
        norm_ffn_g[0], router_wg[0], router_bg[0], router_we[0], router_be[0], exp_w_gu[0],
        exp_w_down[0])
    out = _combine(x1, norm_final_g, y_rows, ids, wcol, offsets)
    return out.reshape(bsz, seq, d)
```

```python
import functools

import jax
import jax.numpy as jnp
from jax import lax
from jax.experimental import pallas as pl
from jax.experimental.pallas import tpu as pltpu

NORM_EPS = 1e-6
LRU_C = 8.0
POOL_WINDOWS = (2, 4, 8, 16)
TOP_K = 2

VMEM_LIMIT_BYTES = 56 * 1024 * 1024
SUBLANES = 8
LANES = 128

MM_TM = 512
MM_TN = 1024
NORM_TM = 512
LRU_TM = 1024
LRU_PARTS = 4
POOL_T = 512
ROUTER_TM = 256
EXPERT_BLOCK = 256
DISPATCH_TM = 1024
EXPERTS_VMEM_LIMIT_BYTES = 63 * 1024 * 1024
COMBINE_TM = 256
COMBINE_PIECES = 8

f32 = jnp.float32
bf16 = jnp.bfloat16


def _cparams(*sem):
    return pltpu.CompilerParams(dimension_semantics=sem, vmem_limit_bytes=VMEM_LIMIT_BYTES)


def _rmsnorm_kernel(x_ref, g_ref, o_ref):
    x = x_ref[...]
    ms = jnp.mean(x * x, axis=-1, keepdims=True)
    o_ref[...] = (x * lax.rsqrt(ms + NORM_EPS) * g_ref[...]).astype(o_ref.dtype)


def _rmsnorm(x, g, out_dtype):
    n, d = x.shape
    return pl.pallas_call(
        _rmsnorm_kernel,
        grid=(n // NORM_TM,),
        in_specs=[pl.BlockSpec((NORM_TM, d), lambda i: (i, 0)),
                  pl.BlockSpec((1, d), lambda i: (0, 0))],
        out_specs=pl.BlockSpec((NORM_TM, d), lambda i: (i, 0)),
        out_shape=jax.ShapeDtypeStruct((n, d), out_dtype),
        compiler_params=_cparams("arbitrary"),
        name="rmsnorm",
    )(x, g.reshape(1, d))


def _mm_kernel(*refs, n_extra, epilogue):
    lhs_ref, w_ref = refs[0], refs[1]
    extras = refs[2:2 + n_extra]
    o_ref = refs[2 + n_extra]
    wbf_ref = refs[3 + n_extra]

    @pl.when(pl.program_id(1) == 0)
    def _():
        wbf_ref[...] = w_ref[...].astype(bf16)

    acc = jnp.dot(lhs_ref[...], wbf_ref[...], preferred_element_type=f32)
    o_ref[...] = epilogue(acc, *[e[...] for e in extras]).astype(o_ref.dtype)


def _matmul(lhs, w, col_off, ncols, out_dtype, epilogue, tm, tn, extras=(), extra_specs=(),
            name="mm"):
    m, k = lhs.shape
    assert m % tm == 0 and ncols % tn == 0 and col_off % tn == 0
    off = col_off // tn
    return pl.pallas_call(
        functools.partial(_mm_kernel, n_extra=len(extras), epilogue=epilogue),
        grid=(ncols // tn, m // tm),
        in_specs=[pl.BlockSpec((tm, k), lambda j, i: (i, 0)),
                  pl.BlockSpec((k, tn), lambda j, i: (0, j + off), pipeline_mode=pl.Buffered(1))]
        + list(extra_specs),
        out_specs=pl.BlockSpec((tm, tn), lambda j, i: (i, j)),
        out_shape=jax.ShapeDtypeStruct((m, ncols), out_dtype),
        scratch_shapes=[pltpu.VMEM((k, tn), bf16)],
        compiler_params=_cparams("arbitrary", "arbitrary"),
        name=name,
    )(lhs, w, *extras)


def _lru_part(acc, tails, carry, cw, cb, wr, br, wi, bi, c_sp, u_scr, g_scr, o_scr):
    rows, c = acc.shape[0], acc.shape[1] // 2
    groups, kw, n_slabs = rows // SUBLANES, cw.shape[0], c // LANES
    grow = lax.broadcasted_iota(jnp.int32, (groups, LANES), 0)
    sub3 = lax.broadcasted_iota(jnp.int32, (groups // SUBLANES, SUBLANES, LANES), 1)
    sub8 = lax.broadcasted_iota(jnp.int32, (SUBLANES, LANES), 0)

    def piece(scr, s, j):
        return scr[s, pl.ds(j, groups, stride=SUBLANES), :]

    uc_pieces, new_tails = [], {j: [] for j in tails}
    for s in range(n_slabs):
        lanes = slice(s * LANES, (s + 1) * LANES)
        u_scr[s] = acc[:, lanes]
        g_scr[s] = acc[:, c + s * LANES:c + (s + 1) * LANES]
        u = [piece(u_scr, s, j) for j in range(SUBLANES)]
        before = {}
        for j in tails:
            before[j] = jnp.where(grow == 0, tails[j][:, lanes], pltpu.roll(u[j], 1, 0))
            new_tails[j].append(u[j][groups - 1:groups])
        ucs = []
        for j in range(SUBLANES):
            v = u[j] * cw[kw - 1:kw, lanes] + cb[:, lanes]
            for k in range(1, kw):
                src = u[j - k] if j >= k else before[j - k + SUBLANES]
                v = v + src * cw[kw - 1 - k:kw - k, lanes]
            ucs.append(v)
        uc_pieces.append(jnp.concatenate(ucs, axis=0))
    uc = jnp.concatenate(uc_pieces, axis=1)

    ub = uc.astype(bf16)
    r_logit = jnp.dot(ub, wr, preferred_element_type=f32) + br
    i_logit = jnp.dot(ub, wi, preferred_element_type=f32) + bi
    neg_log_a = jax.nn.sigmoid(r_logit) * c_sp
    a = jnp.exp(-neg_log_a)
    z = jnp.tanh(neg_log_a) * (1.0 + a * a)
    beta = jnp.where(z > 0.0, z * lax.rsqrt(z), 0.0)
    b = beta * (jax.nn.sigmoid(i_logit) * uc)

    carries = []
    for s in range(n_slabs):
        lanes = slice(s * LANES, (s + 1) * LANES)
        ps, hs = [], []
        for j in range(SUBLANES):
            aj = a[j * groups:(j + 1) * groups, lanes]
            bj = b[j * groups:(j + 1) * groups, lanes]
            hs.append(bj if j == 0 else aj * hs[-1] + bj)
            ps.append(aj if j == 0 else aj * ps[-1])
        pt = ps[-1].reshape(groups // SUBLANES, SUBLANES, LANES)
        ht = hs[-1].reshape(groups // SUBLANES, SUBLANES, LANES)
        d = 1
        while d < SUBLANES:
            keep = sub3 >= d
            ht = jnp.where(keep, pt * pltpu.roll(ht, d, 1) + ht, ht)
            pt = jnp.where(keep, pt * pltpu.roll(pt, d, 1), pt)
            d *= 2
        state = carry[:, lanes]
        incoming = []
        for q in range(groups // SUBLANES):
            after = ht[q] + pt[q] * state
            incoming.append(jnp.where(sub8 == 0, state, pltpu.roll(after, 1, 0)))
            state = after[SUBLANES - 1:SUBLANES]
        carries.append(state)
        incoming = jnp.concatenate(incoming, axis=0)
        for j in range(SUBLANES):
            h = hs[j] + ps[j] * incoming
            o_scr[s, pl.ds(j, groups, stride=SUBLANES), :] = h * jax.nn.gelu(piece(g_scr, s, j))
    out = jnp.concatenate([o_scr[s] for s in range(n_slabs)], axis=1)
    new_tails = {j: jnp.concatenate(v, axis=1) for j, v in new_tails.items()}
    return out, new_tails, jnp.concatenate(carries, axis=1)


def _inproj_lru_kernel(xn_ref, wu_ref, wg_ref, cw_ref, cb_ref, wr_ref, br_ref, wi_ref, bi_ref,
                       lam_ref, o_ref, wbf_ref, tail_ref, carry_ref, u_scr, g_scr, o_scr, *,
                       tiles_per_seq, parts):
    hd = wu_ref.shape[1]
    i = pl.program_id(1)

    @pl.when(i == 0)
    def _():
        wbf_ref[:, :hd] = wu_ref[...].astype(bf16)
        wbf_ref[:, hd:] = wg_ref[...].astype(bf16)

    @pl.when(i % tiles_per_seq == 0)
    def _():
        tail_ref[...] = jnp.zeros_like(tail_ref)
        carry_ref[...] = jnp.zeros_like(carry_ref)

    cw, cb = cw_ref[...], cb_ref[...]
    wr, wi = wr_ref[0].astype(bf16), wi_ref[0].astype(bf16)
    br, bi = br_ref[...], bi_ref[...]
    c_sp = LRU_C * jax.nn.softplus(-lam_ref[...])
    kw = cw.shape[0]
    tails = {j: tail_ref[j:j + 1, :] for j in range(SUBLANES - kw + 1, SUBLANES)}
    carry = carry_ref[...]
    rows = xn_ref.shape[0] // parts
    accs = [jnp.dot(xn_ref[p * rows:(p + 1) * rows, :], wbf_ref[...], preferred_element_type=f32)
            for p in range(parts)]
    for p in range(parts):
        hg, tails, carry = _lru_part(accs[p], tails, carry, cw, cb, wr, br, wi, bi, c_sp,
                                     u_scr, g_scr, o_scr)
        o_ref[p * rows:(p + 1) * rows, :] = hg.astype(o_ref.dtype)
    for j, row in tails.items():
        tail_ref[j:j + 1, :] = row
    carry_ref[...] = carry


def _inproj_lru(xn, w_in, col_gate, conv_w, conv_b, wr, br, wi, bi, lam, seq):
    n, k = xn.shape
    heads, hd, _ = wr.shape
    d_lru = heads * hd
    kw = conv_w.shape[0]
    tm = min(LRU_TM, seq)
    assert seq % tm == 0 and col_gate % hd == 0 and (tm // LRU_PARTS) % SUBLANES == 0
    gate_off = col_gate // hd
    vec = lambda h, i: (0, h)
    return pl.pallas_call(
        functools.partial(_inproj_lru_kernel, tiles_per_seq=seq // tm, parts=LRU_PARTS),
        grid=(heads, n // tm),
        in_specs=[pl.BlockSpec((tm, k), lambda h, i: (i, 0)),
                  pl.BlockSpec((k, hd), lambda h, i: (0, h)),
                  pl.BlockSpec((k, hd), lambda h, i: (0, gate_off + h)),
                  pl.BlockSpec((kw, hd), vec),
                  pl.BlockSpec((1, hd), vec),
                  pl.BlockSpec((1, hd, hd), lambda h, i: (h, 0, 0)),
                  pl.BlockSpec((1, hd), vec),
                  pl.BlockSpec((1, hd, hd), lambda h, i: (h, 0, 0)),
                  pl.BlockSpec((1, hd), vec),
                  pl.BlockSpec((1, hd), vec)],
        out_specs=pl.BlockSpec((tm, hd), lambda h, i: (i, h)),
        out_shape=jax.ShapeDtypeStruct((n, d_lru), bf16),
        scratch_shapes=[pltpu.VMEM((k, 2 * hd), bf16), pltpu.VMEM((SUBLANES, hd), f32),
                        pltpu.VMEM((1, hd), f32)]
        + [pltpu.VMEM((hd // LANES, tm // LRU_PARTS, LANES), f32)] * 3,
        compiler_params=_cparams("arbitrary", "arbitrary"),
        name="in_proj_rg_lru",
    )(xn, w_in, w_in, conv_w, conv_b.reshape(1, d_lru), wr, br.reshape(1, d_lru), wi,
      bi.reshape(1, d_lru), lam.reshape(1, d_lru))


_POOL_HALO = 16


def _pool_kernel(v_ref, o_ref, tail_ref):
    t_len, c = v_ref.shape
    gd = c // len(POOL_WINDOWS)
    t = pl.program_id(1)

    @pl.when(t == 0)
    def _():
        tail_ref[...] = jnp.zeros_like(tail_ref)

    v = v_ref[...]
    ext = jnp.concatenate([tail_ref[...], v], axis=0)
    tail_ref[...] = v[t_len - _POOL_HALO:]
    pos1 = t * t_len + lax.broadcasted_iota(jnp.int32, (t_len, 1), 0) + 1
    for g, w in enumerate(POOL_WINDOWS):
        s = ext[:, g * gd:(g + 1) * gd]
        span = 1
        while span < w:
            s = s + pltpu.roll(s, span, 0)
            span *= 2
        count = jnp.minimum(pos1, w).astype(f32)
        o_ref[:, g * gd:(g + 1) * gd] = (
            s[_POOL_HALO:] / count - v[:, g * gd:(g + 1) * gd]).astype(o_ref.dtype)


def _pool_branch(v, bsz, seq):
    n, c = v.shape
    assert max(POOL_WINDOWS) <= _POOL_HALO
    t_len = min(POOL_T, seq)
    nt = seq // t_len
    return pl.pallas_call(
        _pool_kernel,
        grid=(bsz, nt),
        in_specs=[pl.BlockSpec((t_len, c), lambda b, t: (b * nt + t, 0))],
        out_specs=pl.BlockSpec((t_len, c), lambda b, t: (b * nt + t, 0)),
        out_shape=jax.ShapeDtypeStruct((n, c), bf16),
        scratch_shapes=[pltpu.VMEM((_POOL_HALO, c), f32)],
        compiler_params=_cparams("arbitrary", "arbitrary"),
        name="pool",
    )(v)


_HI16 = 0xFFFF0000


def _pack_bf16_pairs(xb):
    half = xb.shape[1] // 2
    bits = lax.bitcast_convert_type(xb.astype(f32), jnp.uint32)
    return (bits[:, :half] >> 16) | (bits[:, half:] & jnp.uint32(_HI16))


def _unpack_bf16_pairs(p):
    lo = lax.bitcast_convert_type(p << 16, f32)
    hi = lax.bitcast_convert_type(p & jnp.uint32(_HI16), f32)
    return jnp.concatenate([lo, hi], axis=1)


def _rows_to_tiles(p):
    return p.reshape(p.shape[0], p.shape[1] // LANES, LANES)


def _tiles_to_rows(t):
    return t.reshape(t.shape[0], t.shape[1] * t.shape[2])


def _router_kernel(x_ref, g_ref, wc_ref, bc_ref, h_ref, ids_ref, wcol_ref, counts_ref, base_ref,
                   w2_ref, *,
                   n_groups, per_group):
    x = x_ref[...]
    ms = jnp.mean(x * x, axis=-1, keepdims=True)
    h = x * lax.rsqrt(ms + NORM_EPS) * g_ref[...]

    hh = h.astype(bf16)
    h_ref[...] = _rows_to_tiles(_pack_bf16_pairs(hh))
    hl = (h - hh.astype(f32)).astype(bf16)

    @pl.when(pl.program_id(0) == 0)
    def _():
        wc = wc_ref[...]
        wh = wc.astype(bf16)
        w2_ref[:, :LANES] = wh
        w2_ref[:, LANES:] = (wc - wh.astype(f32)).astype(bf16)

    w_hi, w_lo = w2_ref[:, :LANES], w2_ref[:, LANES:]
    logits = (jnp.dot(hh, w_hi, preferred_element_type=f32)
              + jnp.dot(hl, w_hi, preferred_element_type=f32)
              + jnp.dot(hh, w_lo, preferred_element_type=f32))
    lt = logits.T + bc_ref[...]

    tm = x.shape[0]
    row = lax.broadcasted_iota(jnp.int32, (per_group, tm), 0)

    def softmax_rows(z):
        e = jnp.exp(z - jnp.max(z, axis=0, keepdims=True))
        return e / jnp.sum(e, axis=0, keepdims=True)

    def top1(p):
        pmax = jnp.max(p, axis=0, keepdims=True)
        idx = jnp.min(jnp.where(p == pmax, row, per_group), axis=0, keepdims=True)
        return pmax, idx

    assert n_groups == per_group == SUBLANES
    g_p, g_idx = top1(softmax_rows(lt[0:n_groups]))
    sel = jnp.zeros((per_group, tm), f32)
    for g in range(n_groups):
        lo = n_groups + g * per_group
        sel = jnp.where(g_idx == g, lt[lo:lo + per_group], sel)
    p = softmax_rows(sel)
    p1, i1 = top1(p)
    p2, i2 = top1(jnp.where(row == i1, -1.0, p))
    den = p1 + p2
    w1 = g_p * p1 / den
    w2 = g_p * p2 / den
    id1 = g_idx * per_group + i1
    id2 = g_idx * per_group + i2
    lane_row = lax.broadcasted_iota(jnp.int32, (LANES, tm), 0)
    wcol_ref[...] = jnp.where(lane_row == 0, w1, jnp.where(lane_row == 1, w2, 0.0)).T

    @pl.when(pl.program_id(0) == 0)
    def _():
        base_ref[...] = jnp.zeros_like(base_ref)

    n_exp = n_groups * per_group
    e_iota = lax.broadcasted_iota(jnp.int32, (n_exp, tm), 0)
    hot1 = e_iota == id1
    hot2 = e_iota == id2
    hot = jnp.concatenate([hot1, hot2], axis=0).astype(f32).astype(bf16)
    src = lax.broadcasted_iota(jnp.int32, (tm, 2 * tm), 0)
    dst = lax.broadcasted_iota(jnp.int32, (tm, 2 * tm), 1)
    tri_ones = ((src < dst) | (dst >= tm)).astype(f32).astype(bf16)
    cnt = jnp.dot(hot, tri_ones, preferred_element_type=f32)
    base = base_ref[...]
    before1 = cnt[:n_exp, :tm] + base
    tot1 = cnt[:n_exp, tm:]
    before2 = cnt[n_exp:, :tm] + base + tot1
    rank1 = jnp.sum(jnp.where(hot1, before1, 0.0), axis=0, keepdims=True).astype(jnp.int32)
    rank2 = jnp.sum(jnp.where(hot2, before2, 0.0), axis=0, keepdims=True).astype(jnp.int32)
    base = base + tot1 + cnt[n_exp:, tm:]
    base_ref[...] = base
    counts_ref[...] = base
    ids_ref[...] = jnp.where(row == 0, id1, jnp.where(row == 1, id2,
                             jnp.where(row == 2, rank1, jnp.where(row == 3, rank2, 0))))


def _router(x1, g, wg, bg, we, be):
    n, d = x1.shape
    n_groups = wg.shape[1]
    per_group = we.shape[1] // n_groups
    n_exp = we.shape[1]
    ncat = n_groups + n_exp
    assert ncat <= LANES and n_exp % SUBLANES == 0
    wc = jnp.pad(jnp.concatenate([wg, we], axis=1), ((0, 0), (0, LANES - ncat)))
    bc = jnp.pad(jnp.concatenate([bg, be]), (0, LANES - ncat)).reshape(LANES, 1)
    tm = min(ROUTER_TM, n)
    return pl.pallas_call(
        functools.partial(_router_kernel, n_groups=n_groups, per_group=per_group),
        grid=(n // tm,),
        in_specs=[pl.BlockSpec((tm, d), lambda i: (i, 0)),
                  pl.BlockSpec((1, d), lambda i: (0, 0)),
                  pl.BlockSpec((d, LANES), lambda i: (0, 0)),
                  pl.BlockSpec((LANES, 1), lambda i: (0, 0))],
        out_specs=[pl.BlockSpec((tm, d // 2 // LANES, LANES), lambda i: (i, 0, 0)),
                   pl.BlockSpec((SUBLANES, tm), lambda i: (0, i)),
                   pl.BlockSpec((tm, LANES), lambda i: (i, 0)),
                   pl.BlockSpec((n_exp, tm), lambda i: (0, 0))],
        out_shape=[jax.ShapeDtypeStruct((n, d // 2 // LANES, LANES), jnp.uint32),
                   jax.ShapeDtypeStruct((SUBLANES, n), jnp.int32),
                   jax.ShapeDtypeStruct((n, LANES), f32),
                   jax.ShapeDtypeStruct((n_exp, tm), f32)],
        scratch_shapes=[pltpu.VMEM((n_exp, tm), f32), pltpu.VMEM((d, 2 * LANES), bf16)],
        compiler_params=_cparams("arbitrary"),
        name="norm_router",
    )(x1, g.reshape(1, d), wc, bc)


def _routing_tables(counts, blk, n_blocks):
    n_experts = counts.shape[0]
    padded = (counts + blk - 1) // blk * blk
    e = jnp.arange(n_experts, dtype=jnp.int32)
    padded_ends = jnp.sum(jnp.where(e[None, :] <= e[:, None], padded[None, :], 0), axis=1)
    padded_offsets = padded_ends - padded
    block_start = jnp.arange(n_blocks, dtype=jnp.int32) * blk
    block_expert = jnp.minimum(
        jnp.sum((padded_ends[None, :] <= block_start[:, None]).astype(jnp.int32), axis=1),
        n_experts - 1)
    n_used = (padded_ends[n_experts - 1:] // blk).astype(jnp.int32)
    b = jnp.arange(n_blocks, dtype=jnp.int32)
    used = b < n_used[0]
    prev_expert = jnp.concatenate([jnp.full((1,), -1, jnp.int32), block_expert[:-1]])
    first = used & (block_expert != prev_expert)
    slot = (jnp.cumsum(first.astype(jnp.int32)) - 1) % 2
    first_pos = jnp.where(first, b, n_blocks)
    next_first = lax.cummin(jnp.concatenate([first_pos[1:], jnp.full((1,), n_blocks, jnp.int32)]),
                            axis=0, reverse=True)
    next_expert = jnp.where(next_first < n_blocks,
                            block_expert[jnp.minimum(next_first, n_blocks - 1)], -1)
    schedule = jnp.stack([block_expert, slot, first.astype(jnp.int32), next_expert]).astype(jnp.int32)
    return padded_offsets, padded_offsets + counts, padded - counts, schedule, n_used


def _tile_index_table(ids, tm):
    n = ids.shape[1]
    rows = 2 * TOP_K
    return ids[:rows].reshape(rows, n // tm, tm).transpose(1, 0, 2).reshape(n // tm, 1, rows * tm)


_DMA_UNROLL = 8


def _dispatch_kernel(off_ref, pad_start_ref, pad_count_ref, idr_ref, h_ref, xs_hbm, zero_ref, sem,
                     zsem):
    tm = h_ref.shape[0]

    def issue(g, carry):
        for u in range(_DMA_UNROLL):
            r = g * _DMA_UNROLL + u
            for k in range(TOP_K):
                dst = off_ref[idr_ref[0, 0, k * tm + r]] + idr_ref[0, 0, (TOP_K + k) * tm + r]
                pltpu.make_async_copy(h_ref.at[pl.ds(r, 1)], xs_hbm.at[pl.ds(dst, 1)],
                                      sem.at[0]).start(priority=k % 2)
        return carry
    lax.fori_loop(0, tm // _DMA_UNROLL, issue, 0)

    @pl.when(pl.program_id(0) == 0)
    def _():
        zero_ref[...] = jnp.zeros_like(zero_ref)

        chunk = zero_ref.shape[0]
        chunks = [chunk >> s for s in range(chunk.bit_length())]

        def pad_copies(e):
            count = pad_count_ref[e]
            for c in chunks:
                offset = count & ~(2 * c - 1)
                yield (count & c) != 0, pltpu.make_async_copy(
                    zero_ref.at[pl.ds(0, c)], xs_hbm.at[pl.ds(pad_start_ref[e] + offset, c)],
                    zsem.at[0])

        def start_expert(e, carry):
            for present, copy in pad_copies(e):
                pl.when(present)(copy.start)
            return carry

        def wait_expert(e, carry):
            for present, copy in pad_copies(e):
                pl.when(present)(copy.wait)
            return carry

        n_experts = pad_start_ref.shape[0]
        lax.fori_loop(0, n_experts, start_expert, 0)
        lax.fori_loop(0, n_experts, wait_expert, 0)

    for _ in range(TOP_K):
        pltpu.make_async_copy(h_ref, xs_hbm.at[pl.ds(0, tm)], sem.at[0]).wait()


def _dispatch(h_packed, ids, padded_offsets, pad_start, pad_count, max_rows, blk):
    n, row_tiles, lanes = h_packed.shape
    tm = min(DISPATCH_TM, n)
    idr = _tile_index_table(ids, tm)
    grid_spec = pltpu.PrefetchScalarGridSpec(
        num_scalar_prefetch=3,
        grid=(n // tm,),
        in_specs=[pl.BlockSpec((1, 1, 2 * TOP_K * tm), lambda i, *_: (i, 0, 0),
                               memory_space=pltpu.SMEM),
                  pl.BlockSpec((tm, row_tiles, lanes), lambda i, *_: (i, 0, 0))],
        out_specs=pl.BlockSpec(memory_space=pl.ANY),
        scratch_shapes=[pltpu.VMEM((pl.next_power_of_2(blk) // 2, row_tiles, lanes), jnp.uint32),
                        pltpu.SemaphoreType.DMA((1,)), pltpu.SemaphoreType.DMA((1,))],
    )
    return pl.pallas_call(
        _dispatch_kernel,
        grid_spec=grid_spec,
        out_shape=jax.ShapeDtypeStruct((max_rows, row_tiles, lanes), jnp.uint32),
        compiler_params=_cparams("arbitrary"),
        name="dispatch",
    )(padded_offsets, pad_start, pad_count, idr, h_packed)


def _moe_kernel(bexp_ref, slot_ref, first_ref, next_ref, nused_ref, x_ref, wgu_hbm, wdn_hbm, o_ref,
                wgu_buf, wdn_buf, sem):
    b = pl.program_id(0)
    n_used = nused_ref[0]

    def weight_copies(e, s):
        return (pltpu.make_async_copy(wgu_hbm.at[e], wgu_buf.at[s], sem.at[s, 0]),
                pltpu.make_async_copy(wdn_hbm.at[e], wdn_buf.at[s], sem.at[s, 1]))

    @pl.when(b == 0)
    def _():
        for c in weight_copies(bexp_ref[0], 0):
            c.start()

    @pl.when(first_ref[b] == 1)
    def _():
        for c in weight_copies(bexp_ref[b], slot_ref[b]):
            c.wait()

        @pl.when(next_ref[b] >= 0)
        def _():
            for c in weight_copies(next_ref[b], 1 - slot_ref[b]):
                c.start()

    @pl.when(b < n_used)
    def _():
        s = slot_ref[b]
        x = _unpack_bf16_pairs(_tiles_to_rows(x_ref[...])).astype(bf16)
        gu = jnp.dot(x, wgu_buf[s].astype(bf16), preferred_element_type=f32)
        f = gu.shape[1] // 2
        act = (jax.nn.silu(gu[:, :f]) * gu[:, f:]).astype(bf16)
        y = jnp.dot(act, wdn_buf[s].astype(bf16), preferred_element_type=f32)
        o_ref[...] = _rows_to_tiles(_pack_bf16_pairs(y.astype(bf16)))

    @pl.when(b >= n_used)
    def _():
        o_ref[...] = jnp.zeros_like(o_ref)


def _experts(x_sorted, w_gu, w_down, schedule, n_used, blk):
    max_rows, row_tiles, lanes = x_sorted.shape
    _, d, f2 = w_gu.shape
    assert d == 2 * row_tiles * lanes
    nb = max_rows // blk
    row_block = (blk, row_tiles, lanes)
    grid_spec = pltpu.PrefetchScalarGridSpec(
        num_scalar_prefetch=5,
        grid=(nb,),
        in_specs=[
            pl.BlockSpec(row_block, lambda b, be, sl, fi, nx, nu: (jnp.minimum(b, nu[0] - 1), 0, 0)),
            pl.BlockSpec(memory_space=pl.ANY),
            pl.BlockSpec(memory_space=pl.ANY),
        ],
        out_specs=pl.BlockSpec(row_block, lambda b, be, sl, fi, nx, nu: (jnp.minimum(b, nu[0]), 0, 0)),
        scratch_shapes=[pltpu.VMEM((2, d, f2), f32), pltpu.VMEM((2, f2 // 2, d), f32),
                        pltpu.SemaphoreType.DMA((2, 2))],
    )
    return pl.pallas_call(
        _moe_kernel,
        grid_spec=grid_spec,
        out_shape=jax.ShapeDtypeStruct(x_sorted.shape, jnp.uint32),
        compiler_params=pltpu.CompilerParams(dimension_semantics=("arbitrary",),
                                             vmem_limit_bytes=EXPERTS_VMEM_LIMIT_BYTES),
        name="experts",
    )(schedule[0], schedule[1], schedule[2], schedule[3], n_used, x_sorted, w_gu, w_down)


def _start_expert_row_copy(off_ref, idr_ref, tm, r, y_hbm, buf, sem, slot):
    for k in range(TOP_K):
        src = off_ref[idr_ref[0, 0, k * tm + r]] + idr_ref[0, 0, (TOP_K + k) * tm + r]
        pltpu.make_async_copy(y_hbm.at[pl.ds(src, 1)], buf.at[slot, pl.ds(k * tm + r, 1)],
                              sem.at[slot]).start(priority=k % 2)


def _combine_kernel(off_ref, idr_cur_ref, idr_nxt_ref, x_ref, g_ref, w_ref, y_hbm, o_ref, buf, sem):
    i = pl.program_id(0)
    last = pl.num_programs(0) - 1
    tm = x_ref.shape[0]
    slot = i % 2

    def wait_slot(s):
        pltpu.make_async_copy(y_hbm.at[pl.ds(0, buf.shape[1])], buf.at[s], sem.at[s]).wait()

    @pl.when(i == 0)
    def _():
        def issue(g, carry):
            for u in range(_DMA_UNROLL):
                _start_expert_row_copy(off_ref, idr_cur_ref, tm, g * _DMA_UNROLL + u, y_hbm, buf, sem, 0)
            return carry
        lax.fori_loop(0, tm // _DMA_UNROLL, issue, 0)

    wait_slot(slot)
    rc = tm // COMBINE_PIECES
    for c in range(COMBINE_PIECES):
        for r in range(c * rc, (c + 1) * rc):
            _start_expert_row_copy(off_ref, idr_nxt_ref, tm, r, y_hbm, buf, sem, 1 - slot)
        rows = slice(c * rc, (c + 1) * rc)
        w = w_ref[rows, :]
        y = (w[:, 0:1] * _unpack_bf16_pairs(_tiles_to_rows(buf[slot, c * rc:(c + 1) * rc]))
             + w[:, 1:2] * _unpack_bf16_pairs(_tiles_to_rows(buf[slot, tm + c * rc:tm + (c + 1) * rc])))
        x = x_ref[rows, :] + y
        ms = jnp.mean(x * x, axis=-1, keepdims=True)
        o_ref[rows, :] = x * lax.rsqrt(ms + NORM_EPS) * g_ref[...]

    @pl.when(i == last)
    def _():
        wait_slot(1 - slot)


def _combine(x1, g, y_rows, ids, wcol, padded_offsets):
    n, d = x1.shape
    _, row_tiles, lanes = y_rows.shape
    tm = min(COMBINE_TM, n)
    nt = n // tm
    idr = _tile_index_table(ids, tm)
    grid_spec = pltpu.PrefetchScalarGridSpec(
        num_scalar_prefetch=1,
        grid=(nt,),
        in_specs=[
            pl.BlockSpec((1, 1, 2 * TOP_K * tm), lambda i, off: (i, 0, 0), memory_space=pltpu.SMEM),
            pl.BlockSpec((1, 1, 2 * TOP_K * tm), lambda i, off: (jnp.minimum(i + 1, nt - 1), 0, 0),
                         memory_space=pltpu.SMEM),
            pl.BlockSpec((tm, d), lambda i, off: (i, 0)),
            pl.BlockSpec((1, d), lambda i, off: (0, 0)),
            pl.BlockSpec((tm, LANES), lambda i, off: (i, 0)),
            pl.BlockSpec(memory_space=pl.ANY),
        ],
        out_specs=pl.BlockSpec((tm, d), lambda i, off: (i, 0)),
        scratch_shapes=[pltpu.VMEM((2, TOP_K * tm, row_tiles, lanes), jnp.uint32),
                        pltpu.SemaphoreType.DMA((2,))],
    )
    return pl.pallas_call(
        _combine_kernel,
        grid_spec=grid_spec,
        out_shape=jax.ShapeDtypeStruct((n, d), f32),
        compiler_params=_cparams("arbitrary"),
        name="combine_norm",
    )(padded_offsets, idr, idr, x1, g.reshape(1, d), wcol, y_rows)


def _layer(x2, bsz, seq, norm_mix_g, w_in, b_gate, conv_w, conv_b, lru_wr, lru_br, lru_wi, lru_bi,
           lru_lambda, w_lru_out, pool_w, pool_scale, w_out, norm_ffn_g, router_wg, router_bg,
           router_we, router_be, exp_w_gu, exp_w_down):
    n, d = x2.shape
    d_lru = w_lru_out.shape[0]
    n_pool_groups, pool_gd, pool_od = pool_w.shape
    d_pool = n_pool_groups * pool_gd
    col_gate, col_pool, col_merge = d_lru, 2 * d_lru, 2 * d_lru + d_pool
    tn = min(MM_TN, pool_od, d_pool)
    tm = min(MM_TM, n)

    xn = _rmsnorm(x2, norm_mix_g, bf16)
    hg = _inproj_lru(xn, w_in, col_gate, conv_w, conv_b, lru_wr, lru_br, lru_wi, lru_bi, lru_lambda,
                     seq)
    v = _matmul(xn, w_in, col_pool, d_pool, f32, lambda acc: acc, tm, tn, name="in_proj_pool")
    mg = _matmul(xn, w_in, col_merge, 2 * d, bf16,
                 lambda acc, b: jax.nn.sigmoid(acc + b), tm, tn,
                 extras=(b_gate.reshape(1, 2 * d),),
                 extra_specs=(pl.BlockSpec((1, tn), lambda j, i: (0, j)),),
                 name="in_proj_merge")

    pooled = _pool_branch(v, bsz, seq)

    assert pool_od % tn == 0
    per_g = pool_od // tn

    def merge_epilogue(acc, pooled_t, pw, ps, mga, mgb):
        yb = jnp.dot(pooled_t, pw[0].astype(bf16), preferred_element_type=f32) * ps
        return mga.astype(f32) * acc + mgb.astype(f32) * yb

    merged = _matmul(
        hg, w_lru_out, 0, d, bf16, merge_epilogue, tm, tn,
        extras=(pooled, pool_w, pool_scale.reshape(1, d), mg, mg),
        extra_specs=(pl.BlockSpec((tm, pool_gd), lambda j, i: (i, j // per_g)),
                     pl.BlockSpec((1, pool_gd, tn), lambda j, i: (j // per_g, 0, j % per_g)),
                     pl.BlockSpec((1, tn), lambda j, i: (0, j)),
                     pl.BlockSpec((tm, tn), lambda j, i: (i, j)),
                     pl.BlockSpec((tm, tn), lambda j, i: (i, j + d // tn))),
        name="lru_out_merge")

    x1 = _matmul(merged, w_out, 0, d, f32, lambda acc, xr: xr + acc, tm, tn,
                 extras=(x2,), extra_specs=(pl.BlockSpec((tm, tn), lambda j, i: (i, j)),),
                 name="out_proj")

    h2, ids, wcol, counts_f = _router(x1, norm_ffn_g, router_wg, router_bg, router_we, router_be)
    n_experts = exp_w_gu.shape[0]
    max_rows = n * TOP_K + n_experts * EXPERT_BLOCK
    offsets, pad_start, pad_count, schedule, n_used = _routing_tables(
        counts_f[:, 0].astype(jnp.int32), EXPERT_BLOCK, max_rows // EXPERT_BLOCK)
    x_sorted = _dispatch(h2, ids, offsets, pad_start, pad_count, max_rows, EXPERT_BLOCK)
    y_rows = _experts(x_sorted, exp_w_gu, exp_w_down, schedule, n_used, EXPERT_BLOCK)
    return x1, y_rows, ids, wcol, offsets


def kernel(x, norm_mix_g, w_in, b_gate, conv_w, conv_b, lru_wr, lru_br, lru_wi, lru_bi, lru_lambda,
           w_lru_out, pool_w, pool_scale, w_out, norm_ffn_g, router_wg, router_bg, router_we,
           router_be, exp_w_gu, exp_w_down, norm_final_g):
    bsz, seq, d = x.shape
    depth = w_in.shape[0]
    assert depth == 1
    x2 = x.reshape(bsz * seq, d)
    x1, y_rows, ids, wcol, offsets = _layer(
        x2, bsz, seq, norm_mix_g[0], w_in[0], b_gate[0], conv_w[0], conv_b[0], lru_wr[0], lru_br[0],
        lru_wi[0], lru_bi[0], lru_lambda[0], w_lru_out[0], pool_w[0], pool_scale[0], w_out[0],
        norm_ffn_g[0], router_wg[0], router_bg[0], router_we[0], router_be[0], exp_w_gu[0],
        exp_w_down[0])
    out = _combine(x1, norm_final_g, y_rows, ids, wcol, offsets)
    return out.reshape(bsz, seq, d)
```
